```python
import math
import jax, jax.numpy as jnp
from jax import lax
import numpy as np

D_MODEL = 1024
BATCH = 32
SEQ = 2048
DEPTH = 1

CHUNK = 64
MIX_WIDTH = D_MODEL
CONV_WIDTH = MIX_WIDTH // 2
CONV_KSIZE = 31
RW_WIDTH = MIX_WIDTH - CONV_WIDTH
RW_HEAD_DIM = 64
RW_HEADS = RW_WIDTH // RW_HEAD_DIM
LORA_W = 64
LORA_A = 64
LORA_G = 128
RW_PROJ = 3 * RW_WIDTH + LORA_W + LORA_A + LORA_G
IN_PROJ = 2 * CONV_WIDTH + RW_PROJ
PEER_HEADS = 8
PEER_NKEYS = 128
PEER_TOPK = 16
PEER_QDIM = 256
PEER_N_EXPERTS = PEER_NKEYS * PEER_NKEYS
PEER_TOKEN_BLOCK = 128
ALPHA = (2.0 * DEPTH) ** 0.25
BETA = (8.0 * DEPTH) ** -0.25
LN_EPS = 1e-5
GN_EPS = 64e-5

kernel_name = "hybrid_conv_rwkv7_peer_deepnorm_adaln"


def layer_norm(x, g=None, b=None, eps=LN_EPS):
    xf = x.astype(jnp.float32)
    mu = jnp.mean(xf, axis=-1, keepdims=True)
    var = jnp.mean(jnp.square(xf - mu), axis=-1, keepdims=True)
    y = (xf - mu) * lax.rsqrt(var + eps)
    if g is not None:
        y = y * g + b
    return y


def adaln(x, shift, scale):
    return layer_norm(x) * (1.0 + scale[:, None, :]) + shift[:, None, :]


def rwkv7_scan(r, decay, k, v, kk, a):
    bsz, _, nh, nd = r.shape
    xs = tuple(jnp.moveaxis(t.astype(jnp.float32), 1, 0) for t in (r, decay, k, v, kk, a))

    def step(state, inp):
        r_t, w_t, k_t, v_t, kk_t, a_t = inp
        sa = jnp.einsum('bhvk,bhk->bhv', state, -kk_t)
        state = (state * w_t[:, :, None, :]
                 + sa[..., None] * (kk_t * a_t)[:, :, None, :]
                 + v_t[..., None] * k_t[:, :, None, :])
        y_t = jnp.einsum('bhvk,bhk->bhv', state, r_t)
        return state, y_t

    state0 = jnp.zeros((bsz, nh, nd, nd), jnp.float32)
    _, ys = lax.scan(step, state0, xs)
    return jnp.moveaxis(ys, 0, 1)


def setup_inputs(seed: int = 0) -> dict:
    key = jax.random.key(seed)
    ks = jax.random.split(key, 40)
    f32 = jnp.float32

    def nrm(k, shape, scale):
        return jax.random.normal(k, shape, f32) * scale

    s_in = D_MODEL ** -0.5
    col_scale = s_in * jnp.concatenate([
        jnp.full((CONV_WIDTH,), BETA, f32),
        jnp.ones((CONV_WIDTH,), f32),
        jnp.ones((2 * RW_WIDTH,), f32),
        jnp.full((RW_WIDTH,), BETA, f32),
        jnp.ones((LORA_W + LORA_A + LORA_G,), f32),
    ])
    return {
        "x": nrm(ks[0], (BATCH, SEQ, D_MODEL), 1.0),
        "c": nrm(ks[1], (BATCH, D_MODEL), 1.0),
        "cond_w": nrm(ks[2], (D_MODEL, 6 * D_MODEL), 0.5 * D_MODEL ** -0.5),
        "cond_b": nrm(ks[3], (6 * D_MODEL,), 0.02),
        "w_in": jax.random.normal(ks[4], (D_MODEL, IN_PROJ), f32) * col_scale,
        "mu_shift": jax.random.uniform(ks[5], (RW_PROJ,), f32),
        "conv_w": nrm(ks[6], (CONV_KSIZE, CONV_WIDTH), CONV_KSIZE ** -0.5),
        "conv_b": nrm(ks[7], (CONV_WIDTH,), 0.02),
        "conv_ln_g": 1.0 + nrm(ks[8], (CONV_WIDTH,), 0.05),
        "conv_ln_b": nrm(ks[9], (CONV_WIDTH,), 0.02),
        "rw_w0": jnp.linspace(-7.0, -2.0, RW_WIDTH, dtype=f32) + nrm(ks[10], (RW_WIDTH,), 0.1),
        "rw_w2": nrm(ks[11], (LORA_W, RW_WIDTH), 0.1 * LORA_W ** -0.5),
        "rw_a0": nrm(ks[12], (RW_WIDTH,), 0.1),
        "rw_a2": nrm(ks[13], (LORA_A, RW_WIDTH), LORA_A ** -0.5),
        "rw_g2": nrm(ks[14], (LORA_G, RW_WIDTH), LORA_G ** -0.5),
        "rw_kk": 0.85 + nrm(ks[15], (RW_WIDTH,), 0.05),
        "rw_ka": 1.0 + nrm(ks[16], (RW_WIDTH,), 0.05),
        "rw_rk": nrm(ks[17], (RW_HEADS, RW_HEAD_DIM), 0.1),
        "rw_lnx_g": 1.0 + nrm(ks[18], (RW_WIDTH,), 0.05),
        "rw_lnx_b": nrm(ks[19], (RW_WIDTH,), 0.02),
        "w_out": nrm(ks[20], (MIX_WIDTH, D_MODEL), BETA * MIX_WIDTH ** -0.5),
        "ln1_g": 1.0 + nrm(ks[21], (D_MODEL,), 0.05),
        "ln1_b": nrm(ks[22], (D_MODEL,), 0.02),
        "peer_wq": nrm(ks[23], (D_MODEL, PEER_HEADS * PEER_QDIM), s_in),
        "peer_k1": nrm(ks[24], (PEER_HEADS, PEER_NKEYS, PEER_QDIM // 2), (PEER_QDIM // 2) ** -0.5),
        "peer_k2": nrm(ks[25], (PEER_HEADS, PEER_NKEYS, PEER_QDIM // 2), (PEER_QDIM // 2) ** -0.5),
        "peer_u": nrm(ks[26], (PEER_N_EXPERTS, D_MODEL), BETA * s_in),
        "peer_v": nrm(ks[27], (PEER_N_EXPERTS, D_MODEL), BETA),
        "ln2_g": 1.0 + nrm(ks[28], (D_MODEL,), 0.05),
        "ln2_b": nrm(ks[29], (D_MODEL,), 0.02),
    }


def reference(x, c, cond_w, cond_b, w_in, mu_shift, conv_w, conv_b, conv_ln_g, conv_ln_b,
              rw_w0, rw_w2, rw_a0, rw_a2, rw_g2, rw_kk, rw_ka, rw_rk, rw_lnx_g, rw_lnx_b,
              w_out, ln1_g, ln1_b, peer_wq, peer_k1, peer_k2, peer_u, peer_v, ln2_g, ln2_b):
    bsz, seq, dm = x.shape

    mod = jax.nn.silu(c) @ cond_w + cond_b
    shift1, scale1, gate1, shift2, scale2, gate2 = jnp.split(mod, 6, axis=-1)

    for _ in range(DEPTH):
        h = adaln(x, shift1, scale1)
        p = h @ w_in
        p_conv = p[..., :2 * CONV_WIDTH]
        p_rw = p[..., 2 * CONV_WIDTH:]

        u = p_conv[..., :CONV_WIDTH] * jax.nn.sigmoid(p_conv[..., CONV_WIDTH:])
        u = lax.conv_general_dilated(
            u, conv_w.astype(u.dtype)[:, None, :], window_strides=(1,),
            padding=[(CONV_KSIZE - 1, 0)], dimension_numbers=('NWC', 'WIO', 'NWC'),
            feature_group_count=CONV_WIDTH) + conv_b
        y_conv = jax.nn.silu(layer_norm(u, conv_ln_g, conv_ln_b))

        p_prev = jnp.pad(p_rw[:, :-1], ((0, 0), (1, 0), (0, 0)))
        xm = p_rw + (p_prev - p_rw) * mu_shift
        o = 0
        r = xm[..., o:o + RW_WIDTH]; o += RW_WIDTH
        k = xm[..., o:o + RW_WIDTH]; o += RW_WIDTH
        v = xm[..., o:o + RW_WIDTH]; o += RW_WIDTH
        wd = xm[..., o:o + LORA_W]; o += LORA_W
        ad = xm[..., o:o + LORA_A]; o += LORA_A
        gd = xm[..., o:o + LORA_G]
        w_log = -jax.nn.softplus(-(rw_w0 + jnp.tanh(wd) @ rw_w2)) - 0.5
        decay = jnp.exp(-jnp.exp(w_log.astype(jnp.float32)))
        a = jax.nn.sigmoid(rw_a0 + ad @ rw_a2)
        g = jax.nn.sigmoid(gd) @ rw_g2
        heads = lambda t: t.reshape(bsz, seq, RW_HEADS, RW_HEAD_DIM)
        kk = heads(k * rw_kk).astype(jnp.float32)
        kk = kk / jnp.maximum(jnp.linalg.norm(kk, axis=-1, keepdims=True), 1e-12)
        k = k * (1.0 + (a - 1.0) * rw_ka)
        rh, kh, vh, ah = heads(r), heads(k), heads(v), heads(a)
        y_rw = rwkv7_scan(rh, heads(decay), kh, vh, kk, ah)
        y_rw = layer_norm(y_rw, eps=GN_EPS).reshape(bsz, seq, RW_WIDTH) * rw_lnx_g + rw_lnx_b
        bonus = jnp.sum(rh * kh * rw_rk, axis=-1, keepdims=True) * vh
        y_rw = (y_rw + bonus.reshape(bsz, seq, RW_WIDTH)) * g

        y1 = jnp.concatenate([y_conv, y_rw], axis=-1) @ w_out
        x = layer_norm(ALPHA * x + gate1[:, None, :] * y1, ln1_g, ln1_b)

        h2 = adaln(x, shift2, scale2)
        q = (h2 @ peer_wq).reshape(bsz, seq, PEER_HEADS, PEER_QDIM).astype(jnp.float32)
        half = PEER_QDIM // 2
        s1 = jnp.einsum('bshd,hnd->bshn', q[..., :half], peer_k1.astype(jnp.float32))
        s2 = jnp.einsum('bshd,hnd->bshn', q[..., half:], peer_k2.astype(jnp.float32))
        v1, i1 = lax.top_k(s1, PEER_TOPK)
        v2, i2 = lax.top_k(s2, PEER_TOPK)
        cand = (v1[..., :, None] + v2[..., None, :]).reshape(bsz, seq, PEER_HEADS, PEER_TOPK * PEER_TOPK)
        sc, ci = lax.top_k(cand, PEER_TOPK)
        experts = (jnp.take_along_axis(i1, ci // PEER_TOPK, axis=-1) * PEER_NKEYS
                   + jnp.take_along_axis(i2, ci % PEER_TOPK, axis=-1))
        gates = jax.nn.softmax(sc, axis=-1)

        n_tok = bsz * seq
        nblk = n_tok // PEER_TOKEN_BLOCK
        hk = PEER_HEADS * PEER_TOPK
        h_blk = h2.reshape(nblk, PEER_TOKEN_BLOCK, dm)
        e_blk = experts.reshape(nblk, PEER_TOKEN_BLOCK, hk)
        g_blk = gates.reshape(nblk, PEER_TOKEN_BLOCK, hk)

        def peer_block(args):
            hb, eb, gb = args
            u_sel = jnp.take(peer_u, eb, axis=0)
            z = jnp.einsum('td,tkd->tk', hb, u_sel)
            act = jax.nn.gelu(z, approximate=False) * gb
            return jnp.einsum('tk,tkd->td', act, jnp.take(peer_v, eb, axis=0))

        y2 = lax.map(peer_block, (h_blk, e_blk, g_blk)).reshape(bsz, seq, dm)
        x = layer_norm(ALPHA * x + gate2[:, None, :] * y2, ln2_g, ln2_b)

    return x
```

```python
import functools
import math

import jax
import jax.numpy as jnp
from jax import lax
from jax.experimental import pallas as pl
from jax.experimental.pallas import tpu as pltpu

F32 = jnp.float32
BF16 = jnp.bfloat16
HIGHEST = lax.Precision.HIGHEST

LN_EPS = 1e-5
GN_EPS = 64e-5
PEER_TOPK = 16
DEPTH = 1
LANES = 128
ROW_TILE = 256
SCAN_CHUNK = 64
PEER_TOK = 128
PACK_ROWS = 512
VMEM_LIMIT = 56 * 1024 * 1024


def _cparams(sem):
    return pltpu.CompilerParams(dimension_semantics=sem, vmem_limit_bytes=VMEM_LIMIT)


def _dot(a, b, precision=None):
    return jnp.dot(a, b, preferred_element_type=F32, precision=precision)


def _sigmoid(x):
    return 1.0 / (1.0 + jnp.exp(-x))


def _ln(x, eps):
    mu = jnp.mean(x, axis=-1, keepdims=True)
    xc = x - mu
    var = jnp.mean(xc * xc, axis=-1, keepdims=True)
    return xc * lax.rsqrt(var + eps)


def _const_spec(shape):
    nd = len(shape)
    return pl.BlockSpec(shape, lambda *_: (0,) * nd, pipeline_mode=pl.Buffered(1))


def _mod_kernel(c_ref, w_ref, b_ref, o_ref):
    c = c_ref[...]
    o_ref[...] = _dot(c * _sigmoid(c), w_ref[...], HIGHEST) + b_ref[...]


def _mod_call(c, cond_w, cond_b):
    bsz, dm = c.shape
    nblk = cond_w.shape[1] // dm
    return pl.pallas_call(
        _mod_kernel,
        out_shape=jax.ShapeDtypeStruct((bsz, nblk * dm), F32),
        grid=(nblk,),
        in_specs=[pl.BlockSpec((bsz, dm), lambda j: (0, 0)),
                  pl.BlockSpec((dm, dm), lambda j: (0, j)),
                  pl.BlockSpec((1, dm), lambda j: (0, j))],
        out_specs=pl.BlockSpec((bsz, dm), lambda j: (0, j)),
        compiler_params=_cparams(("arbitrary",)),
    )(c, cond_w, cond_b.reshape(1, -1))


def _front_kernel(x_ref, mod_ref, win_ref, mu_ref, convw_ref, convb_ref, cg_ref, cb_ref,
                  w0_ref, a0_ref, wa2_ref, g2_ref, kkw_ref, kaw_ref, rk_ref, bd_ref,
                  rw_ref, yconv_ref, gb_ref, ubuf_ref, prev_ref, *, dm, cw, rw, ksize, halo):
    tm = x_ref.shape[1]
    first = pl.program_id(1) == 0

    @pl.when(first)
    def _():
        ubuf_ref[0:halo, :] = jnp.zeros((halo, cw), F32)
        prev_ref[...] = jnp.zeros(prev_ref.shape, F32)

    x = x_ref[0]
    mod = mod_ref[0]
    shift1 = mod[:, 0:dm]
    scale1 = mod[:, dm:2 * dm]
    h = _ln(x, LN_EPS) * (1.0 + scale1) + shift1
    p = _dot(h.astype(BF16), win_ref[...])

    u = p[:, 0:cw] * _sigmoid(p[:, cw:2 * cw])
    ubuf_ref[halo:halo + tm, :] = u
    acc = jnp.zeros((tm, cw), F32) + convb_ref[...]
    off = halo - (ksize - 1)
    for j in range(ksize):
        acc = acc + convw_ref[j:j + 1, :] * ubuf_ref[off + j:off + j + tm, :]
    ubuf_ref[0:halo, :] = ubuf_ref[tm:tm + halo, :]
    yc = _ln(acc, LN_EPS) * cg_ref[...] + cb_ref[...]
    yconv_ref[0] = yc * _sigmoid(yc)

    prw = p[:, 2 * cw:]
    rolled = pltpu.roll(prw, 1, 0)
    row = lax.broadcasted_iota(jnp.int32, prw.shape, 0)
    p_prev = jnp.where(row == 0, prev_ref[0:1, :], rolled)
    prev_ref[0:1, :] = prw[tm - 1:tm, :]
    xm = prw + (p_prev - prw) * mu_ref[...]
    r = xm[:, 0:rw]
    k = xm[:, rw:2 * rw]
    v = xm[:, 2 * rw:3 * rw]
    wa = xm[:, 3 * rw:3 * rw + LANES]
    gd = xm[:, 3 * rw + LANES:3 * rw + 2 * LANES]
    lane = lax.broadcasted_iota(jnp.int32, wa.shape, 1)
    wa = jnp.where(lane < LANES // 2, jnp.tanh(wa), wa)
    t12 = _dot(wa, wa2_ref[...], HIGHEST)
    g = _dot(_sigmoid(gd), g2_ref[...], HIGHEST)
    y = w0_ref[...] + t12[:, 0:rw]
    sp = jnp.maximum(-y, 0.0) + jnp.log(1.0 + jnp.exp(-jnp.abs(y)))
    lw = -jnp.exp(-sp - 0.5)
    a = _sigmoid(a0_ref[...] + t12[:, rw:2 * rw])
    kk = k * kkw_ref[...]
    ss = _dot(kk * kk, bd_ref[...], HIGHEST)
    kk = kk / jnp.maximum(jnp.sqrt(ss), 1e-12)
    kmod = k * (1.0 + (a - 1.0) * kaw_ref[...])
    bonus = _dot(r * kmod * rk_ref[...], bd_ref[...], HIGHEST) * v
    rw_ref[0, :, 0:rw] = r
    rw_ref[0, :, rw:2 * rw] = kmod
    rw_ref[0, :, 2 * rw:3 * rw] = v
    rw_ref[0, :, 3 * rw:4 * rw] = kk
    rw_ref[0, :, 4 * rw:5 * rw] = kk * a
    rw_ref[0, :, 5 * rw:6 * rw] = lw
    gb_ref[0, :, 0:rw] = g
    gb_ref[0, :, rw:2 * rw] = bonus


def _front_call(x, mod3, w_in, mu_shift, conv_w, conv_b, cg, cb, w0, a0, wa2, g2, kkw, kaw,
                rk, bd, tm):
    bsz, seq, dm = x.shape
    ksize, cw = conv_w.shape
    rw = w0.shape[-1]
    halo = 32
    assert ksize - 1 <= halo <= tm
    row = lambda a: a.reshape(1, -1)
    consts = [w_in, row(mu_shift), conv_w, row(conv_b), row(cg), row(cb), row(w0), row(a0),
              wa2, g2, row(kkw), row(kaw), row(rk), bd]
    kern = functools.partial(_front_kernel, dm=dm, cw=cw, rw=rw, ksize=ksize, halo=halo)
    return pl.pallas_call(
        kern,
        out_shape=(jax.ShapeDtypeStruct((bsz, seq, 6 * rw), F32),
                   jax.ShapeDtypeStruct((bsz, seq, cw), F32),
                   jax.ShapeDtypeStruct((bsz, seq, 2 * rw), F32)),
        grid=(bsz, seq // tm),
        in_specs=[pl.BlockSpec((1, tm, dm), lambda b, s: (b, s, 0)),
                  pl.BlockSpec((1, 1, mod3.shape[-1]), lambda b, s: (b, 0, 0))]
                 + [_const_spec(a.shape) for a in consts],
        out_specs=(pl.BlockSpec((1, tm, 6 * rw), lambda b, s: (b, s, 0)),
                   pl.BlockSpec((1, tm, cw), lambda b, s: (b, s, 0)),
                   pl.BlockSpec((1, tm, 2 * rw), lambda b, s: (b, s, 0))),
        scratch_shapes=[pltpu.VMEM((tm + halo, cw), F32),
                        pltpu.VMEM((8, w_in.shape[1] - 2 * cw), F32)],
        compiler_params=_cparams(("arbitrary", "arbitrary")),
    )(x, mod3, *consts)


def _bmm(a, b):
    return lax.dot_general(a, b, (((2,), (1,)), ((0,), (0,))),
                           preferred_element_type=F32, precision=HIGHEST)


def _bmm_nt(a, b):
    return lax.dot_general(a, b, (((2,), (2,)), ((0,), (0,))),
                           preferred_element_type=F32, precision=HIGHEST)


def _bmm_tn(a, b):
    return lax.dot_general(a, b, (((1,), (1,)), ((0,), (0,))),
                           preferred_element_type=F32, precision=HIGHEST)


def _scan_kernel(rw_ref, tri_ref, y_ref, state_ref, *, nh, hd):
    L = rw_ref.shape[1]
    rwid = nh * hd

    @pl.when(pl.program_id(1) == 0)
    def _():
        state_ref[...] = jnp.zeros(state_ref.shape, F32)

    blk = rw_ref[0]
    tri = tri_ref[...]
    c = _dot(tri, blk[:, 5 * rwid:6 * rwid], HIGHEST)
    lw = blk[:, 5 * rwid:6 * rwid]
    e_pos = jnp.exp(c)
    e_neg = jnp.exp(-c)
    rt_all = blk[:, 0:rwid] * e_pos
    kh_all = blk[:, rwid:2 * rwid] * e_neg
    kt_all = blk[:, 3 * rwid:4 * rwid] * jnp.exp(c - lw)
    bh_all = blk[:, 4 * rwid:5 * rwid] * e_neg
    v_all = blk[:, 2 * rwid:3 * rwid]
    pl_all = e_pos[L - 1:L, :]

    def heads(t):
        return jnp.stack([t[:, h * hd:(h + 1) * hd] for h in range(nh)], axis=0)

    rt, kh, kt, bh, v = heads(rt_all), heads(kh_all), heads(kt_all), heads(bh_all), heads(v_all)
    plh = heads(pl_all)
    s0 = state_ref[...]

    strict = (tri - jnp.eye(L, dtype=F32))[None]
    incl = tri[None]
    t_bb = _bmm_nt(kt, bh) * strict
    t_bk = _bmm_nt(kt, kh) * strict
    g_rb = _bmm_nt(rt, bh) * incl
    g_rk = _bmm_nt(rt, kh) * incl

    w = _bmm_nt(kt, s0) + _bmm(t_bk, v)
    w = w - _bmm(t_bb, w)
    tp = t_bb
    n = 2
    while n < L:
        tp = _bmm(tp, tp)
        w = w + _bmm(tp, w)
        n *= 2
    u = w
    y = _bmm_nt(rt, s0) - _bmm(g_rb, u) + _bmm(g_rk, v)
    state_ref[...] = s0 * plh + _bmm_tn(v, kh * plh) - _bmm_tn(u, bh * plh)
    y_ref[0] = jnp.concatenate([y[h] for h in range(nh)], axis=-1)


def _scan_call(rwpack, nh, hd, chunk):
    bsz, seq, w6 = rwpack.shape
    rwid = nh * hd
    tri = jnp.tril(jnp.ones((chunk, chunk), F32))
    kern = functools.partial(_scan_kernel, nh=nh, hd=hd)
    return pl.pallas_call(
        kern,
        out_shape=jax.ShapeDtypeStruct((bsz, seq, rwid), F32),
        grid=(bsz, seq // chunk),
        in_specs=[pl.BlockSpec((1, chunk, w6), lambda b, s: (b, s, 0)),
                  _const_spec(tri.shape)],
        out_specs=pl.BlockSpec((1, chunk, rwid), lambda b, s: (b, s, 0)),
        scratch_shapes=[pltpu.VMEM((nh, hd, hd), F32)],
        compiler_params=_cparams(("arbitrary", "arbitrary")),
    )(rwpack, tri)


def _extract_max(s, iota, fill):
    m = jnp.max(s, axis=0, keepdims=True)
    idx = jnp.min(jnp.where(s == m, iota, fill), axis=0, keepdims=True)
    return m, idx, jnp.where(iota == idx, -jnp.inf, s)


def _topk_cols(s, k):
    n = s.shape[0]
    iota = lax.broadcasted_iota(jnp.int32, s.shape, 0)
    vals, idxs = [], []
    for _ in range(k):
        m, idx, s = _extract_max(s, iota, n)
        vals.append(m)
        idxs.append(idx)
    return jnp.concatenate(vals, axis=0), jnp.concatenate(idxs, axis=0)


def _post_kernel(x_ref, y_ref, yconv_ref, gb_ref, mod_ref, wout_ref, lnxg_ref, lnxb_ref,
                 ln1g_ref, ln1b_ref, wq_ref, k1_ref, k2_ref, bd_ref,
                 x1_ref, h2_ref, e_ref, gate_ref, q_ref, *, dm, rw, hd, alpha, nkeys, topk):
    tm = x_ref.shape[1]
    nheads = k1_ref.shape[0]
    half = k1_ref.shape[2]
    x = x_ref[0]
    mod = mod_ref[0]
    gate1 = mod[:, 2 * dm:3 * dm]
    shift2 = mod[:, 3 * dm:4 * dm]
    scale2 = mod[:, 4 * dm:5 * dm]

    y = y_ref[0]
    bd = bd_ref[...]
    mu = _dot(y, bd, HIGHEST) * (1.0 / hd)
    yc = y - mu
    var = _dot(yc * yc, bd, HIGHEST) * (1.0 / hd)
    gn = yc * lax.rsqrt(var + GN_EPS)
    gb = gb_ref[0]
    y_rw = (gn * lnxg_ref[...] + lnxb_ref[...] + gb[:, rw:2 * rw]) * gb[:, 0:rw]
    ycat = jnp.concatenate([yconv_ref[0], y_rw], axis=-1)
    y1 = _dot(ycat.astype(BF16), wout_ref[...])
    x1 = _ln(alpha * x + gate1 * y1, LN_EPS) * ln1g_ref[...] + ln1b_ref[...]
    x1_ref[0] = x1
    h2 = _ln(x1, LN_EPS) * (1.0 + scale2) + shift2
    h2_ref[0] = h2
    q = _dot(h2.astype(BF16), wq_ref[...])
    for i in range(2 * nheads):
        q_ref[i] = q[:, i * half:(i + 1) * half]

    def head_body(h, carry):
        s1 = lax.dot_general(k1_ref[h], q_ref[2 * h], (((1,), (1,)), ((), ())),
                             preferred_element_type=F32, precision=HIGHEST)
        s2 = lax.dot_general(k2_ref[h], q_ref[2 * h + 1], (((1,), (1,)), ((), ())),
                             preferred_element_type=F32, precision=HIGHEST)
        v1, i1 = _topk_cols(s1, topk)
        v2, i2 = _topk_cols(s2, topk)
        cand = jnp.concatenate([v1[a:a + 1, :] + v2 for a in range(topk)], axis=0)
        eall = jnp.concatenate([i1[a:a + 1, :] * nkeys + i2 for a in range(topk)], axis=0)
        iota = lax.broadcasted_iota(jnp.int32, cand.shape, 0)
        scs, exs = [], []
        for _ in range(topk):
            m, idx, cand_next = _extract_max(cand, iota, topk * topk)
            exs.append(jnp.max(jnp.where(iota == idx, eall, -1), axis=0, keepdims=True))
            scs.append(m)
            cand = cand_next
        sc = jnp.concatenate(scs, axis=0)
        ex = jnp.concatenate(exs, axis=0)
        pexp = jnp.exp(sc - sc[0:1, :])
        gates = pexp / jnp.sum(pexp, axis=0, keepdims=True)
        base = pl.multiple_of(h * topk, topk)
        e_ref[pl.ds(base, topk), :] = ex
        gate_ref[pl.ds(base, topk), :] = gates
        return carry

    lax.fori_loop(0, nheads, head_body, 0)


def _post_call(x, yscan, yconv, gb, mod3, w_out, lnxg, lnxb, ln1g, ln1b, wq, k1, k2, bd,
               hd, alpha, topk, tm):
    bsz, seq, dm = x.shape
    rw = yscan.shape[-1]
    cw = yconv.shape[-1]
    nheads, nkeys, half = k1.shape
    nslot = nheads * topk
    ntok = bsz * seq
    nst = seq // tm
    row = lambda a: a.reshape(1, -1)
    consts = [w_out, row(lnxg), row(lnxb), row(ln1g), row(ln1b), wq, k1, k2, bd]
    kern = functools.partial(_post_kernel, dm=dm, rw=rw, hd=hd, alpha=alpha, nkeys=nkeys,
                             topk=topk)
    tok = lambda w: pl.BlockSpec((1, tm, w), lambda b, s: (b, s, 0))
    slot = pl.BlockSpec((nslot, tm), lambda b, s: (0, b * nst + s))
    return pl.pallas_call(
        kern,
        out_shape=(jax.ShapeDtypeStruct((bsz, seq, dm), F32),
                   jax.ShapeDtypeStruct((bsz, seq, dm), F32),
                   jax.ShapeDtypeStruct((nslot, ntok), jnp.int32),
                   jax.ShapeDtypeStruct((nslot, ntok), F32)),
        grid=(bsz, nst),
        in_specs=[tok(dm), tok(rw), tok(cw), tok(2 * rw),
                  pl.BlockSpec((1, 1, mod3.shape[-1]), lambda b, s: (b, 0, 0))]
                 + [_const_spec(a.shape) for a in consts],
        out_specs=(tok(dm), tok(dm), slot, slot),
        scratch_shapes=[pltpu.VMEM((2 * nheads, tm, half), F32)],
        compiler_params=_cparams(("arbitrary", "arbitrary")),
    )(x, yscan, yconv, gb, mod3, *consts)


def _pack_kernel(t_ref, o_ref):
    t = t_ref[...]
    hw = t.shape[1] // 2
    hi = lax.bitcast_convert_type(t[:, 0:hw].astype(BF16).astype(F32), jnp.uint32)
    lo = lax.bitcast_convert_type(t[:, hw:].astype(BF16).astype(F32), jnp.uint32)
    o_ref[...] = (hi & jnp.uint32(0xFFFF0000)) | (lo >> 16)


def _pack_call(table):
    ne, dm = table.shape
    packed = pl.pallas_call(
        _pack_kernel,
        out_shape=jax.ShapeDtypeStruct((ne, dm // 2), jnp.uint32),
        grid=(ne // PACK_ROWS,),
        in_specs=[pl.BlockSpec((PACK_ROWS, dm), lambda i: (i, 0))],
        out_specs=pl.BlockSpec((PACK_ROWS, dm // 2), lambda i: (i, 0)),
        compiler_params=_cparams(("arbitrary",)),
    )(table)
    return packed.reshape(ne, dm // (2 * LANES), LANES)


def _unpack(w):
    hi = lax.bitcast_convert_type(w & jnp.uint32(0xFFFF0000), F32)
    lo = lax.bitcast_convert_type(w << 16, F32)
    return hi, lo


def _peer_u_kernel(e_ref, h_ref, tbl_ref, z_ref, p_ref, *, nslot, nj):
    tm = h_ref.shape[0]
    lane = lax.broadcasted_iota(jnp.int32, z_ref.shape, 1)
    z_ref[...] = jnp.zeros(z_ref.shape, F32)

    def tok_body(t, carry):
        hv = h_ref[t]
        h_hi = hv[0:nj]
        h_lo = hv[nj:2 * nj]
        for k in range(nslot):
            hi, lo = _unpack(tbl_ref[e_ref[k, t]])
            p_ref[k * nj:(k + 1) * nj, :] = hi * h_hi + lo * h_lo
        rs = jnp.sum(p_ref[...], axis=1, keepdims=True)
        z_ref[...] = jnp.where(lane == t, rs, z_ref[...])
        return carry

    lax.fori_loop(0, tm, tok_body, 0)


def _peer_u_call(experts, h3, tbl, tm):
    nslot, ntok = experts.shape
    ne, nj, _ = tbl.shape
    kern = functools.partial(_peer_u_kernel, nslot=nslot, nj=nj)
    return pl.pallas_call(
        kern,
        out_shape=jax.ShapeDtypeStruct((nslot * nj, ntok), F32),
        grid=(ntok // tm,),
        in_specs=[pl.BlockSpec((nslot, tm), lambda i: (0, i), memory_space=pltpu.SMEM),
                  pl.BlockSpec((tm, 2 * nj, LANES), lambda i: (i, 0, 0)),
                  _const_spec(tbl.shape)],
        out_specs=pl.BlockSpec((nslot * nj, tm), lambda i: (0, i)),
        scratch_shapes=[pltpu.VMEM((nslot * nj, LANES), F32)],
        compiler_params=_cparams(("arbitrary",)),
    )(experts, h3, tbl)


def _act_kernel(z_ref, sel_ref, gate_ref, a_ref):
    z = _dot(sel_ref[...], z_ref[...], HIGHEST)
    a_ref[...] = 0.5 * z * (1.0 + lax.erf(z * (1.0 / math.sqrt(2.0)))) * gate_ref[...]


def _act_call(zpart, gates, tm):
    nslot, ntok = gates.shape
    nj = zpart.shape[0] // nslot
    sel = (jnp.arange(nslot * nj)[None, :] // nj == jnp.arange(nslot)[:, None]).astype(F32)
    return pl.pallas_call(
        _act_kernel,
        out_shape=jax.ShapeDtypeStruct((nslot, ntok), F32),
        grid=(ntok // tm,),
        in_specs=[pl.BlockSpec((nslot * nj, tm), lambda i: (0, i)),
                  _const_spec(sel.shape),
                  pl.BlockSpec((nslot, tm), lambda i: (0, i))],
        out_specs=pl.BlockSpec((nslot, tm), lambda i: (0, i)),
        compiler_params=_cparams(("arbitrary",)),
    )(zpart, sel, gates)


def _peer_v_kernel(e_ref, a_ref, tbl_ref, y_ref, *, nslot, nj, nacc):
    tm = y_ref.shape[0]

    def tok_body(t, carry):
        acc_hi = [jnp.zeros((nj, LANES), F32) for _ in range(nacc)]
        acc_lo = [jnp.zeros((nj, LANES), F32) for _ in range(nacc)]
        for k in range(nslot):
            hi, lo = _unpack(tbl_ref[e_ref[k, t]])
            a = a_ref[k, t]
            acc_hi[k % nacc] = acc_hi[k % nacc] + a * hi
            acc_lo[k % nacc] = acc_lo[k % nacc] + a * lo
        y_ref[t] = jnp.concatenate([sum(acc_hi[1:], acc_hi[0]), sum(acc_lo[1:], acc_lo[0])],
                                   axis=0)
        return carry

    lax.fori_loop(0, tm, tok_body, 0)


def _peer_v_call(experts, act, tbl, tm):
    nslot, ntok = experts.shape
    ne, nj, _ = tbl.shape
    kern = functools.partial(_peer_v_kernel, nslot=nslot, nj=nj, nacc=4)
    smem = lambda: pl.BlockSpec((nslot, tm), lambda i: (0, i), memory_space=pltpu.SMEM)
    return pl.pallas_call(
        kern,
        out_shape=jax.ShapeDtypeStruct((ntok, 2 * nj, LANES), F32),
        grid=(ntok // tm,),
        in_specs=[smem(), smem(), _const_spec(tbl.shape)],
        out_specs=pl.BlockSpec((tm, 2 * nj, LANES), lambda i: (i, 0, 0)),
        compiler_params=_cparams(("arbitrary",)),
    )(experts, act, tbl)


def _final_kernel(x_ref, y_ref, mod_ref, g_ref, b_ref, o_ref, *, dm, alpha):
    gate2 = mod_ref[0][:, 5 * dm:6 * dm]
    o_ref[0] = _ln(alpha * x_ref[0] + gate2 * y_ref[0], LN_EPS) * g_ref[...] + b_ref[...]


def _final_call(x1, y2, mod3, g, b, alpha, tm):
    bsz, seq, dm = x1.shape
    tok = pl.BlockSpec((1, tm, dm), lambda bi, s: (bi, s, 0))
    return pl.pallas_call(
        functools.partial(_final_kernel, dm=dm, alpha=alpha),
        out_shape=jax.ShapeDtypeStruct((bsz, seq, dm), F32),
        grid=(bsz, seq // tm),
        in_specs=[tok, tok, pl.BlockSpec((1, 1, mod3.shape[-1]), lambda bi, s: (bi, 0, 0)),
                  _const_spec((1, dm)), _const_spec((1, dm))],
        out_specs=tok,
        compiler_params=_cparams(("arbitrary", "arbitrary")),
    )(x1, y2, mod3, g.reshape(1, -1), b.reshape(1, -1))


def kernel(x, c, cond_w, cond_b, w_in, mu_shift, conv_w, conv_b, conv_ln_g, conv_ln_b, rw_w0, rw_w2, rw_a0, rw_a2, rw_g2, rw_kk, rw_ka, rw_rk, rw_lnx_g, rw_lnx_b, w_out, ln1_g, ln1_b, peer_wq, peer_k1, peer_k2, peer_u, peer_v, ln2_g, ln2_b):
    bsz, seq, dm = x.shape
    nh, hd = rw_rk.shape
    rw = nh * hd
    lora_w, lora_a = rw_w2.shape[0], rw_a2.shape[0]
    assert lora_w == lora_a == LANES // 2 and rw_g2.shape[0] == LANES
    topk = PEER_TOPK
    alpha = (2.0 * DEPTH) ** 0.25
    tm = min(ROW_TILE, seq)
    chunk = min(SCAN_CHUNK, seq)

    mod3 = _mod_call(c, cond_w, cond_b).reshape(bsz, 1, -1)

    head_id = jnp.arange(rw) // hd
    bd = (head_id[:, None] == head_id[None, :]).astype(F32)
    wa2 = jnp.zeros((LANES, 2 * rw), F32)
    wa2 = wa2.at[0:lora_w, 0:rw].set(rw_w2).at[lora_w:, rw:].set(rw_a2)

    rwpack, yconv, gb = _front_call(
        x, mod3, w_in.astype(BF16), mu_shift, conv_w, conv_b, conv_ln_g, conv_ln_b, rw_w0,
        rw_a0, wa2, rw_g2, rw_kk, rw_ka, rw_rk.reshape(-1), bd, tm)
    yscan = _scan_call(rwpack, nh, hd, chunk)
    x1, h2, experts, gates = _post_call(
        x, yscan, yconv, gb, mod3, w_out.astype(BF16), rw_lnx_g, rw_lnx_b, ln1_g, ln1_b,
        peer_wq.astype(BF16), peer_k1, peer_k2, bd, hd, alpha, topk, tm)

    ntok = bsz * seq
    ptm = min(PEER_TOK, ntok)
    h3 = h2.reshape(ntok, dm // LANES, LANES)
    zpart = _peer_u_call(experts, h3, _pack_call(peer_u), ptm)
    act = _act_call(zpart, gates, ptm)
    y3 = _peer_v_call(experts, act, _pack_call(peer_v), ptm)
    y2 = y3.reshape(bsz, seq, dm)
    return _final_call(x1, y2, mod3, ln2_g, ln2_b, alpha, tm)
```

```python
import functools
import math

import jax
import jax.numpy as jnp
from jax import lax
from jax.experimental import pallas as pl
from jax.experimental.pallas import tpu as pltpu

F32 = jnp.float32
BF16 = jnp.bfloat16
HIGHEST = lax.Precision.HIGHEST

LN_EPS = 1e-5
GN_EPS = 64e-5
PEER_TOPK = 16
DEPTH = 1
LANES = 128
ROW_TILE = 256
SCAN_CHUNK = 64
PEER_TOK = 128
GATHER_CHUNK = 64
PACK_ROWS = 512
VMEM_LIMIT = 56 * 1024 * 1024


def _cparams(sem):
    return pltpu.CompilerParams(dimension_semantics=sem, vmem_limit_bytes=VMEM_LIMIT)


def _dot(a, b, precision=None):
    return jnp.dot(a, b, preferred_element_type=F32, precision=precision)


def _sigmoid(x):
    return 1.0 / (1.0 + jnp.exp(-x))


def _ln(x, eps):
    mu = jnp.mean(x, axis=-1, keepdims=True)
    xc = x - mu
    var = jnp.mean(xc * xc, axis=-1, keepdims=True)
    return xc * lax.rsqrt(var + eps)


def _const_spec(shape):
    nd = len(shape)
    return pl.BlockSpec(shape, lambda *_: (0,) * nd, pipeline_mode=pl.Buffered(1))


def _mod_kernel(c_ref, w_ref, b_ref, o_ref):
    c = c_ref[...]
    o_ref[...] = _dot(c * _sigmoid(c), w_ref[...], HIGHEST) + b_ref[...]


def _mod_call(c, cond_w, cond_b):
    bsz, dm = c.shape
    nblk = cond_w.shape[1] // dm
    return pl.pallas_call(
        _mod_kernel,
        name="mod",
        out_shape=jax.ShapeDtypeStruct((bsz, nblk * dm), F32),
        grid=(nblk,),
        in_specs=[pl.BlockSpec((bsz, dm), lambda j: (0, 0)),
                  pl.BlockSpec((dm, dm), lambda j: (0, j)),
                  pl.BlockSpec((1, dm), lambda j: (0, j))],
        out_specs=pl.BlockSpec((bsz, dm), lambda j: (0, j)),
        compiler_params=_cparams(("arbitrary",)),
    )(c, cond_w, cond_b.reshape(1, -1))


def _front_kernel(x_ref, mod_ref, win_ref, mu_ref, convw_ref, convb_ref, cg_ref, cb_ref,
                  w0_ref, a0_ref, wa2_ref, g2_ref, kkw_ref, kaw_ref, rk_ref, bd_ref,
                  rw_ref, yconv_ref, gb_ref, ubuf_ref, prev_ref, *, dm, cw, rw, ksize, halo):
    tm = x_ref.shape[1]
    first = pl.program_id(1) == 0

    @pl.when(first)
    def _():
        ubuf_ref[0:halo, :] = jnp.zeros((halo, cw), F32)
        prev_ref[...] = jnp.zeros(prev_ref.shape, F32)

    x = x_ref[0]
    mod = mod_ref[0]
    shift1 = mod[:, 0:dm]
    scale1 = mod[:, dm:2 * dm]
    h = _ln(x, LN_EPS) * (1.0 + scale1) + shift1
    p = _dot(h.astype(BF16), win_ref[...])

    u = p[:, 0:cw] * _sigmoid(p[:, cw:2 * cw])
    ubuf_ref[halo:halo + tm, :] = u
    acc = jnp.zeros((tm, cw), F32) + convb_ref[...]
    off = halo - (ksize - 1)
    for j in range(ksize):
        acc = acc + convw_ref[j:j + 1, :] * ubuf_ref[off + j:off + j + tm, :]
    ubuf_ref[0:halo, :] = ubuf_ref[tm:tm + halo, :]
    yc = _ln(acc, LN_EPS) * cg_ref[...] + cb_ref[...]
    yconv_ref[0] = yc * _sigmoid(yc)

    prw = p[:, 2 * cw:]
    rolled = pltpu.roll(prw, 1, 0)
    row = lax.broadcasted_iota(jnp.int32, prw.shape, 0)
    p_prev = jnp.where(row == 0, prev_ref[0:1, :], rolled)
    prev_ref[0:1, :] = prw[tm - 1:tm, :]
    xm = prw + (p_prev - prw) * mu_ref[...]
    r = xm[:, 0:rw]
    k = xm[:, rw:2 * rw]
    v = xm[:, 2 * rw:3 * rw]
    wa = xm[:, 3 * rw:3 * rw + LANES]
    gd = xm[:, 3 * rw + LANES:3 * rw + 2 * LANES]
    lane = lax.broadcasted_iota(jnp.int32, wa.shape, 1)
    wa = jnp.where(lane < LANES // 2, jnp.tanh(wa), wa)
    t12 = _dot(wa, wa2_ref[...], HIGHEST)
    g = _dot(_sigmoid(gd), g2_ref[...], HIGHEST)
    y = w0_ref[...] + t12[:, 0:rw]
    sp = jnp.maximum(-y, 0.0) + jnp.log(1.0 + jnp.exp(-jnp.abs(y)))
    lw = -jnp.exp(-sp - 0.5)
    a = _sigmoid(a0_ref[...] + t12[:, rw:2 * rw])
    kk = k * kkw_ref[...]
    ss = _dot(kk * kk, bd_ref[...], HIGHEST)
    kk = kk / jnp.maximum(jnp.sqrt(ss), 1e-12)
    kmod = k * (1.0 + (a - 1.0) * kaw_ref[...])
    bonus = _dot(r * kmod * rk_ref[...], bd_ref[...], HIGHEST) * v
    rw_ref[0, :, 0:rw] = r
    rw_ref[0, :, rw:2 * rw] = kmod
    rw_ref[0, :, 2 * rw:3 * rw] = v
    rw_ref[0, :, 3 * rw:4 * rw] = kk
    rw_ref[0, :, 4 * rw:5 * rw] = kk * a
    rw_ref[0, :, 5 * rw:6 * rw] = lw
    gb_ref[0, :, 0:rw] = g
    gb_ref[0, :, rw:2 * rw] = bonus


def _front_call(x, mod3, w_in, mu_shift, conv_w, conv_b, cg, cb, w0, a0, wa2, g2, kkw, kaw,
                rk, bd, tm):
    bsz, seq, dm = x.shape
    ksize, cw = conv_w.shape
    rw = w0.shape[-1]
    halo = 32
    assert ksize - 1 <= halo <= tm
    row = lambda a: a.reshape(1, -1)
    consts = [w_in, row(mu_shift), conv_w, row(conv_b), row(cg), row(cb), row(w0), row(a0),
              wa2, g2, row(kkw), row(kaw), row(rk), bd]
    kern = functools.partial(_front_kernel, dm=dm, cw=cw, rw=rw, ksize=ksize, halo=halo)
    return pl.pallas_call(
        kern,
        name="front",
        out_shape=(jax.ShapeDtypeStruct((bsz, seq, 6 * rw), F32),
                   jax.ShapeDtypeStruct((bsz, seq, cw), F32),
                   jax.ShapeDtypeStruct((bsz, seq, 2 * rw), F32)),
        grid=(bsz, seq // tm),
        in_specs=[pl.BlockSpec((1, tm, dm), lambda b, s: (b, s, 0)),
                  pl.BlockSpec((1, 1, mod3.shape[-1]), lambda b, s: (b, 0, 0))]
                 + [_const_spec(a.shape) for a in consts],
        out_specs=(pl.BlockSpec((1, tm, 6 * rw), lambda b, s: (b, s, 0)),
                   pl.BlockSpec((1, tm, cw), lambda b, s: (b, s, 0)),
                   pl.BlockSpec((1, tm, 2 * rw), lambda b, s: (b, s, 0))),
        scratch_shapes=[pltpu.VMEM((tm + halo, cw), F32),
                        pltpu.VMEM((8, w_in.shape[1] - 2 * cw), F32)],
        compiler_params=_cparams(("arbitrary", "arbitrary")),
    )(x, mod3, *consts)


def _bmm(a, b):
    return lax.dot_general(a, b, (((2,), (1,)), ((0,), (0,))),
                           preferred_element_type=F32, precision=HIGHEST)


def _bmm_nt(a, b):
    return lax.dot_general(a, b, (((2,), (2,)), ((0,), (0,))),
                           preferred_element_type=F32, precision=HIGHEST)


def _bmm_tn(a, b):
    return lax.dot_general(a, b, (((1,), (1,)), ((0,), (0,))),
                           preferred_element_type=F32, precision=HIGHEST)


def _scan_kernel(rw_ref, tri_ref, y_ref, state_ref, *, nh, hd):
    L = rw_ref.shape[1]
    rwid = nh * hd

    @pl.when(pl.program_id(1) == 0)
    def _():
        state_ref[...] = jnp.zeros(state_ref.shape, F32)

    blk = rw_ref[0]
    tri = tri_ref[...]
    c = _dot(tri, blk[:, 5 * rwid:6 * rwid], HIGHEST)
    lw = blk[:, 5 * rwid:6 * rwid]
    e_pos = jnp.exp(c)
    e_neg = jnp.exp(-c)
    rt_all = blk[:, 0:rwid] * e_pos
    kh_all = blk[:, rwid:2 * rwid] * e_neg
    kt_all = blk[:, 3 * rwid:4 * rwid] * jnp.exp(c - lw)
    bh_all = blk[:, 4 * rwid:5 * rwid] * e_neg
    v_all = blk[:, 2 * rwid:3 * rwid]
    pl_all = e_pos[L - 1:L, :]

    def heads(t):
        return jnp.stack([t[:, h * hd:(h + 1) * hd] for h in range(nh)], axis=0)

    rt, kh, kt, bh, v = heads(rt_all), heads(kh_all), heads(kt_all), heads(bh_all), heads(v_all)
    plh = heads(pl_all)
    s0 = state_ref[...]

    strict = (tri - jnp.eye(L, dtype=F32))[None]
    incl = tri[None]
    t_bb = _bmm_nt(kt, bh) * strict
    t_bk = _bmm_nt(kt, kh) * strict
    g_rb = _bmm_nt(rt, bh) * incl
    g_rk = _bmm_nt(rt, kh) * incl

    w = _bmm_nt(kt, s0) + _bmm(t_bk, v)
    w = w - _bmm(t_bb, w)
    tp = t_bb
    n = 2
    while n < L:
        tp = _bmm(tp, tp)
        w = w + _bmm(tp, w)
        n *= 2
    u = w
    y = _bmm_nt(rt, s0) - _bmm(g_rb, u) + _bmm(g_rk, v)
    state_ref[...] = s0 * plh + _bmm_tn(v, kh * plh) - _bmm_tn(u, bh * plh)
    y_ref[0] = jnp.concatenate([y[h] for h in range(nh)], axis=-1)


def _scan_call(rwpack, nh, hd, chunk):
    bsz, seq, w6 = rwpack.shape
    rwid = nh * hd
    tri = jnp.tril(jnp.ones((chunk, chunk), F32))
    kern = functools.partial(_scan_kernel, nh=nh, hd=hd)
    return pl.pallas_call(
        kern,
        name="rwkv_scan",
        out_shape=jax.ShapeDtypeStruct((bsz, seq, rwid), F32),
        grid=(bsz, seq // chunk),
        in_specs=[pl.BlockSpec((1, chunk, w6), lambda b, s: (b, s, 0)),
                  _const_spec(tri.shape)],
        out_specs=pl.BlockSpec((1, chunk, rwid), lambda b, s: (b, s, 0)),
        scratch_shapes=[pltpu.VMEM((nh, hd, hd), F32)],
        compiler_params=_cparams(("arbitrary", "arbitrary")),
    )(rwpack, tri)


def _extract_max(s, iota, fill):
    m = jnp.max(s, axis=0, keepdims=True)
    idx = jnp.min(jnp.where(s == m, iota, fill), axis=0, keepdims=True)
    return m, idx, jnp.where(iota == idx, -jnp.inf, s)


def _topk_cols(s, k):
    n = s.shape[0]
    iota = lax.broadcasted_iota(jnp.int32, s.shape, 0)
    vals, idxs = [], []
    for _ in range(k):
        m, idx, s = _extract_max(s, iota, n)
        vals.append(m)
        idxs.append(idx)
    return jnp.concatenate(vals, axis=0), jnp.concatenate(idxs, axis=0)


def _post_kernel(x_ref, y_ref, yconv_ref, gb_ref, mod_ref, wout_ref, lnxg_ref, lnxb_ref,
                 ln1g_ref, ln1b_ref, wq_ref, k1_ref, k2_ref, bd_ref,
                 x1_ref, h2_ref, e_ref, gate_ref, q_ref, *, dm, rw, hd, alpha, nkeys, topk):
    tm = x_ref.shape[1]
    nheads = k1_ref.shape[0]
    half = k1_ref.shape[2]
    x = x_ref[0]
    mod = mod_ref[0]
    gate1 = mod[:, 2 * dm:3 * dm]
    shift2 = mod[:, 3 * dm:4 * dm]
    scale2 = mod[:, 4 * dm:5 * dm]

    y = y_ref[0]
    bd = bd_ref[...]
    mu = _dot(y, bd, HIGHEST) * (1.0 / hd)
    yc = y - mu
    var = _dot(yc * yc, bd, HIGHEST) * (1.0 / hd)
    gn = yc * lax.rsqrt(var + GN_EPS)
    gb = gb_ref[0]
    y_rw = (gn * lnxg_ref[...] + lnxb_ref[...] + gb[:, rw:2 * rw]) * gb[:, 0:rw]
    ycat = jnp.concatenate([yconv_ref[0], y_rw], axis=-1)
    y1 = _dot(ycat.astype(BF16), wout_ref[...])
    x1 = _ln(alpha * x + gate1 * y1, LN_EPS) * ln1g_ref[...] + ln1b_ref[...]
    x1_ref[0] = x1
    h2 = _ln(x1, LN_EPS) * (1.0 + scale2) + shift2
    h2_ref[0] = h2
    q = _dot(h2.astype(BF16), wq_ref[...])
    for i in range(2 * nheads):
        q_ref[i] = q[:, i * half:(i + 1) * half]

    def head_body(h, carry):
        s1 = lax.dot_general(k1_ref[h], q_ref[2 * h], (((1,), (1,)), ((), ())),
                             preferred_element_type=F32, precision=HIGHEST)
        s2 = lax.dot_general(k2_ref[h], q_ref[2 * h + 1], (((1,), (1,)), ((), ())),
                             preferred_element_type=F32, precision=HIGHEST)
        v1, i1 = _topk_cols(s1, topk)
        v2, i2 = _topk_cols(s2, topk)
        cand = jnp.concatenate([v1[a:a + 1, :] + v2 for a in range(topk)], axis=0)
        eall = jnp.concatenate([i1[a:a + 1, :] * nkeys + i2 for a in range(topk)], axis=0)
        iota = lax.broadcasted_iota(jnp.int32, cand.shape, 0)
        scs, exs = [], []
        for _ in range(topk):
            m, idx, cand_next = _extract_max(cand, iota, topk * topk)
            exs.append(jnp.max(jnp.where(iota == idx, eall, -1), axis=0, keepdims=True))
            scs.append(m)
            cand = cand_next
        sc = jnp.concatenate(scs, axis=0)
        ex = jnp.concatenate(exs, axis=0)
        pexp = jnp.exp(sc - sc[0:1, :])
        gates = pexp / jnp.sum(pexp, axis=0, keepdims=True)
        base = pl.multiple_of(h * topk, topk)
        e_ref[pl.ds(base, topk), :] = ex
        gate_ref[pl.ds(base, topk), :] = gates
        return carry

    lax.fori_loop(0, nheads, head_body, 0)


def _post_call(x, yscan, yconv, gb, mod3, w_out, lnxg, lnxb, ln1g, ln1b, wq, k1, k2, bd,
               hd, alpha, topk, tm):
    bsz, seq, dm = x.shape
    rw = yscan.shape[-1]
    cw = yconv.shape[-1]
    nheads, nkeys, half = k1.shape
    nslot = nheads * topk
    ntok = bsz * seq
    nst = seq // tm
    row = lambda a: a.reshape(1, -1)
    consts = [w_out, row(lnxg), row(lnxb), row(ln1g), row(ln1b), wq, k1, k2, bd]
    kern = functools.partial(_post_kernel, dm=dm, rw=rw, hd=hd, alpha=alpha, nkeys=nkeys,
                             topk=topk)
    tok = lambda w: pl.BlockSpec((1, tm, w), lambda b, s: (b, s, 0))
    slot = pl.BlockSpec((nslot, tm), lambda b, s: (0, b * nst + s))
    return pl.pallas_call(
        kern,
        name="post_route",
        out_shape=(jax.ShapeDtypeStruct((bsz, seq, dm), F32),
                   jax.ShapeDtypeStruct((bsz, seq, dm), F32),
                   jax.ShapeDtypeStruct((nslot, ntok), jnp.int32),
                   jax.ShapeDtypeStruct((nslot, ntok), F32)),
        grid=(bsz, nst),
        in_specs=[tok(dm), tok(rw), tok(cw), tok(2 * rw),
                  pl.BlockSpec((1, 1, mod3.shape[-1]), lambda b, s: (b, 0, 0))]
                 + [_const_spec(a.shape) for a in consts],
        out_specs=(tok(dm), tok(dm), slot, slot),
        scratch_shapes=[pltpu.VMEM((2 * nheads, tm, half), F32)],
        compiler_params=_cparams(("arbitrary", "arbitrary")),
    )(x, yscan, yconv, gb, mod3, *consts)


def _pack_kernel(t_ref, o_ref):
    t = t_ref[...]
    hw = t.shape[1] // 2
    hi = lax.bitcast_convert_type(t[:, 0:hw].astype(BF16).astype(F32), jnp.uint32)
    lo = lax.bitcast_convert_type(t[:, hw:].astype(BF16).astype(F32), jnp.uint32)
    o_ref[...] = (hi & jnp.uint32(0xFFFF0000)) | (lo >> 16)


def _pack_call(table):
    ne, dm = table.shape
    packed = pl.pallas_call(
        _pack_kernel,
        name="peer_pack",
        out_shape=jax.ShapeDtypeStruct((ne, dm // 2), jnp.uint32),
        grid=(ne // PACK_ROWS,),
        in_specs=[pl.BlockSpec((PACK_ROWS, dm), lambda i: (i, 0))],
        out_specs=pl.BlockSpec((PACK_ROWS, dm // 2), lambda i: (i, 0)),
        compiler_params=_cparams(("arbitrary",)),
    )(table)
    return packed.reshape(ne * (dm // (2 * LANES)), LANES)


def _unpack(w):
    hi = lax.bitcast_convert_type(w & jnp.uint32(0xFFFF0000), F32)
    lo = lax.bitcast_convert_type(w << 16, F32)
    return hi, lo


def _gather_pair(tbl_ref, e_ref, off, nj):
    rows = [tbl_ref[pl.ds(pl.multiple_of(e_ref[off + d], nj), nj), :] for d in range(2)]
    return jnp.concatenate(rows, axis=0)


def _peer_u_kernel(e_ref, h_ref, tbl_ref, z_ref, p0_ref, p1_ref, *, nslot, nj):
    tm = h_ref.shape[0]
    rows = 2 * nj
    lane = lax.broadcasted_iota(jnp.int32, (rows, LANES), 1)
    z_ref[...] = jnp.zeros(z_ref.shape, F32)

    @pl.when(pl.program_id(0) == 0)
    def _():
        p1_ref[...] = jnp.zeros(p1_ref.shape, F32)

    def reduce_rows(p_ref, row, tcol):
        rs = jnp.sum(p_ref[row:row + rows, :], axis=1, keepdims=True)
        z_ref[row:row + rows, :] = jnp.where(lane == tcol, rs, z_ref[row:row + rows, :])

    def one_token(t, p_ref, q_ref):
        hv = h_ref[t]
        h_hi = jnp.concatenate([hv[0:nj], hv[0:nj]], axis=0)
        h_lo = jnp.concatenate([hv[nj:2 * nj], hv[nj:2 * nj]], axis=0)
        ev = e_ref.at[pl.ds(t * nslot, nslot)]
        for i in range(0, nslot, 2):
            hi, lo = _unpack(_gather_pair(tbl_ref, ev, i, nj))
            p_ref[i * nj:i * nj + rows, :] = hi * h_hi + lo * h_lo
            reduce_rows(q_ref, i * nj, t - 1)

    def tok_body(i, carry):
        one_token(2 * i, p0_ref, p1_ref)
        one_token(2 * i + 1, p1_ref, p0_ref)
        return carry

    lax.fori_loop(0, tm // 2, tok_body, 0)
    for r in range(0, nslot * nj, rows):
        reduce_rows(p1_ref, r, tm - 1)


def _peer_u_call(experts, h3, tbl, tm):
    ntok = h3.shape[0]
    nslot = experts.shape[0] // ntok
    nj = h3.shape[1] // 2
    kern = functools.partial(_peer_u_kernel, nslot=nslot, nj=nj)
    return pl.pallas_call(
        kern,
        name="peer_u",
        out_shape=jax.ShapeDtypeStruct((nslot * nj, ntok), F32),
        grid=(ntok // tm,),
        in_specs=[pl.BlockSpec((tm * nslot,), lambda i: (i,), memory_space=pltpu.SMEM),
                  pl.BlockSpec((tm, 2 * nj, LANES), lambda i: (i, 0, 0)),
                  _const_spec(tbl.shape)],
        out_specs=pl.BlockSpec((nslot * nj, tm), lambda i: (0, i)),
        scratch_shapes=[pltpu.VMEM((nslot * nj, LANES), F32),
                        pltpu.VMEM((nslot * nj, LANES), F32)],
        compiler_params=_cparams(("arbitrary",)),
    )(experts, h3, tbl)


def _act_kernel(z_ref, sel_ref, gate_ref, a_ref):
    z = _dot(sel_ref[...], z_ref[...], HIGHEST)
    a = 0.5 * z * (1.0 + lax.erf(z * (1.0 / math.sqrt(2.0)))) * gate_ref[...]
    a_ref[...] = a.T


def _act_call(zpart, gates, tm):
    nslot, ntok = gates.shape
    nj = zpart.shape[0] // nslot
    sel = (jnp.arange(nslot * nj)[None, :] // nj == jnp.arange(nslot)[:, None]).astype(F32)
    return pl.pallas_call(
        _act_kernel,
        name="peer_act",
        out_shape=jax.ShapeDtypeStruct((ntok, nslot), F32),
        grid=(ntok // tm,),
        in_specs=[pl.BlockSpec((nslot * nj, tm), lambda i: (0, i)),
                  _const_spec(sel.shape),
                  pl.BlockSpec((nslot, tm), lambda i: (0, i))],
        out_specs=pl.BlockSpec((tm, nslot), lambda i: (i, 0)),
        compiler_params=_cparams(("arbitrary",)),
    )(zpart, sel, gates)


def _peer_v_kernel(e_ref, a_ref, tbl_ref, y_ref, *, nslot, nj, nacc):
    tm = y_ref.shape[0]
    upper = lax.broadcasted_iota(jnp.int32, (2 * nj, LANES), 0) >= nj

    def tok_body(t, carry):
        def chunk_body(c, accs):
            accs = list(accs)
            off = t * nslot + c * GATHER_CHUNK
            ev = e_ref.at[pl.ds(off, GATHER_CHUNK)]
            av = a_ref.at[pl.ds(off, GATHER_CHUNK)]
            for i in range(0, GATHER_CHUNK, 2):
                hi, lo = _unpack(_gather_pair(tbl_ref, ev, i, nj))
                a = jnp.where(upper, av[i + 1], av[i])
                j = (i // 2) % nacc
                accs[2 * j] = accs[2 * j] + a * hi
                accs[2 * j + 1] = accs[2 * j + 1] + a * lo
            return tuple(accs)

        zero = jnp.zeros((2 * nj, LANES), F32)
        accs = lax.fori_loop(0, nslot // GATHER_CHUNK, chunk_body, (zero,) * (2 * nacc))
        acc_hi = sum(accs[2::2], accs[0])
        acc_lo = sum(accs[3::2], accs[1])
        y_ref[t] = jnp.concatenate([acc_hi[0:nj] + acc_hi[nj:], acc_lo[0:nj] + acc_lo[nj:]],
                                   axis=0)
        return carry

    lax.fori_loop(0, tm, tok_body, 0)


def _peer_v_call(experts, act, tbl, ntok, nj, tm):
    nslot = experts.shape[0] // ntok
    kern = functools.partial(_peer_v_kernel, nslot=nslot, nj=nj, nacc=2)
    smem = lambda: pl.BlockSpec((tm * nslot,), lambda i: (i,), memory_space=pltpu.SMEM)
    return pl.pallas_call(
        kern,
        name="peer_v",
        out_shape=jax.ShapeDtypeStruct((ntok, 2 * nj, LANES), F32),
        grid=(ntok // tm,),
        in_specs=[smem(), smem(), _const_spec(tbl.shape)],
        out_specs=pl.BlockSpec((tm, 2 * nj, LANES), lambda i: (i, 0, 0)),
        compiler_params=_cparams(("arbitrary",)),
    )(experts, act, tbl)


def _final_kernel(x_ref, y_ref, mod_ref, g_ref, b_ref, o_ref, *, dm, alpha):
    gate2 = mod_ref[0][:, 5 * dm:6 * dm]
    o_ref[0] = _ln(alpha * x_ref[0] + gate2 * y_ref[0], LN_EPS) * g_ref[...] + b_ref[...]


def _final_call(x1, y2, mod3, g, b, alpha, tm):
    bsz, seq, dm = x1.shape
    tok = pl.BlockSpec((1, tm, dm), lambda bi, s: (bi, s, 0))
    return pl.pallas_call(
        functools.partial(_final_kernel, dm=dm, alpha=alpha),
        name="final_ln",
        out_shape=jax.ShapeDtypeStruct((bsz, seq, dm), F32),
        grid=(bsz, seq // tm),
        in_specs=[tok, tok, pl.BlockSpec((1, 1, mod3.shape[-1]), lambda bi, s: (bi, 0, 0)),
                  _const_spec((1, dm)), _const_spec((1, dm))],
        out_specs=tok,
        compiler_params=_cparams(("arbitrary", "arbitrary")),
    )(x1, y2, mod3, g.reshape(1, -1), b.reshape(1, -1))


def kernel(x, c, cond_w, cond_b, w_in, mu_shift, conv_w, conv_b, conv_ln_g, conv_ln_b, rw_w0, rw_w2, rw_a0, rw_a2, rw_g2, rw_kk, rw_ka, rw_rk, rw_lnx_g, rw_lnx_b, w_out, ln1_g, ln1_b, peer_wq, peer_k1, peer_k2, peer_u, peer_v, ln2_g, ln2_b):
    bsz, seq, dm = x.shape
    nh, hd = rw_rk.shape
    rw = nh * hd
    lora_w, lora_a = rw_w2.shape[0], rw_a2.shape[0]
    assert lora_w == lora_a == LANES // 2 and rw_g2.shape[0] == LANES
    topk = PEER_TOPK
    alpha = (2.0 * DEPTH) ** 0.25
    tm = min(ROW_TILE, seq)
    chunk = min(SCAN_CHUNK, seq)

    mod3 = _mod_call(c, cond_w, cond_b).reshape(bsz, 1, -1)

    head_id = jnp.arange(rw) // hd
    bd = (head_id[:, None] == head_id[None, :]).astype(F32)
    wa2 = jnp.zeros((LANES, 2 * rw), F32)
    wa2 = wa2.at[0:lora_w, 0:rw].set(rw_w2).at[lora_w:, rw:].set(rw_a2)

    rwpack, yconv, gb = _front_call(
        x, mod3, w_in.astype(BF16), mu_shift, conv_w, conv_b, conv_ln_g, conv_ln_b, rw_w0,
        rw_a0, wa2, rw_g2, rw_kk, rw_ka, rw_rk.reshape(-1), bd, tm)
    yscan = _scan_call(rwpack, nh, hd, chunk)
    x1, h2, experts, gates = _post_call(
        x, yscan, yconv, gb, mod3, w_out.astype(BF16), rw_lnx_g, rw_lnx_b, ln1_g, ln1_b,
        peer_wq.astype(BF16), peer_k1, peer_k2, bd, hd, alpha, topk, tm)

    ntok = bsz * seq
    ptm = min(PEER_TOK, ntok)
    h3 = h2.reshape(ntok, dm // LANES, LANES)
    nj = dm // (2 * LANES)
    experts_t = (experts * nj).T.reshape(-1)
    zpart = _peer_u_call(experts_t, h3, _pack_call(peer_u), ptm)
    act = _act_call(zpart, gates, ptm).reshape(-1)
    y3 = _peer_v_call(experts_t, act, _pack_call(peer_v), ntok, nj, ptm)
    y2 = y3.reshape(bsz, seq, dm)
    return _final_call(x1, y2, mod3, ln2_g, ln2_b, alpha, tm)
```

```python
import functools
import math

import jax
import jax.numpy as jnp
from jax import lax
from jax.experimental import pallas as pl
from jax.experimental.pallas import tpu as pltpu

F32 = jnp.float32
BF16 = jnp.bfloat16
HIGHEST = lax.Precision.HIGHEST

LN_EPS = 1e-5
GN_EPS = 64e-5
PEER_TOPK = 16
DEPTH = 1
LANES = 128
ROW_TILE = 256
SCAN_CHUNK = 64
PEER_TOK = 128
GATHER_CHUNK = 64
PACK_ROWS = 512
VMEM_LIMIT = 56 * 1024 * 1024


def _cparams(sem):
    return pltpu.CompilerParams(dimension_semantics=sem, vmem_limit_bytes=VMEM_LIMIT)


def _dot(a, b, precision=None):
    return jnp.dot(a, b, preferred_element_type=F32, precision=precision)


def _sigmoid(x):
    return 1.0 / (1.0 + jnp.exp(-x))


def _ln(x, eps):
    mu = jnp.mean(x, axis=-1, keepdims=True)
    xc = x - mu
    var = jnp.mean(xc * xc, axis=-1, keepdims=True)
    return xc * lax.rsqrt(var + eps)


def _head_sums(x, bd):
    hi = x.astype(BF16)
    lo = (x - hi.astype(F32)).astype(BF16)
    return _dot(hi, bd) + _dot(lo, bd)


def _const_spec(shape):
    nd = len(shape)
    return pl.BlockSpec(shape, lambda *_: (0,) * nd, pipeline_mode=pl.Buffered(1))


def _mod_kernel(c_ref, w_ref, b_ref, o_ref):
    c = c_ref[...]
    o_ref[...] = _dot(c * _sigmoid(c), w_ref[...], HIGHEST) + b_ref[...]


def _mod_call(c, cond_w, cond_b):
    bsz, dm = c.shape
    nblk = cond_w.shape[1] // dm
    return pl.pallas_call(
        _mod_kernel,
        name="mod",
        out_shape=jax.ShapeDtypeStruct((bsz, nblk * dm), F32),
        grid=(nblk,),
        in_specs=[pl.BlockSpec((bsz, dm), lambda j: (0, 0)),
                  pl.BlockSpec((dm, dm), lambda j: (0, j)),
                  pl.BlockSpec((1, dm), lambda j: (0, j))],
        out_specs=pl.BlockSpec((bsz, dm), lambda j: (0, j)),
        compiler_params=_cparams(("arbitrary",)),
    )(c, cond_w, cond_b.reshape(1, -1))


def _front_kernel(x_ref, mod_ref, win_ref, mu_ref, convw_ref, convb_ref, cg_ref, cb_ref,
                  w0_ref, a0_ref, wa2_ref, g2_ref, kkw_ref, kaw_ref, rk_ref, bd_ref,
                  rw_ref, yconv_ref, gb_ref, ubuf_ref, prev_ref, *, dm, cw, rw, ksize, halo):
    tm = x_ref.shape[1]
    first = pl.program_id(1) == 0

    @pl.when(first)
    def _():
        ubuf_ref[0:halo, :] = jnp.zeros((halo, cw), F32)
        prev_ref[...] = jnp.zeros(prev_ref.shape, F32)

    x = x_ref[0]
    mod = mod_ref[0]
    shift1 = mod[:, 0:dm]
    scale1 = mod[:, dm:2 * dm]
    h = _ln(x, LN_EPS) * (1.0 + scale1) + shift1
    p = _dot(h.astype(BF16), win_ref[...])

    u = p[:, 0:cw] * _sigmoid(p[:, cw:2 * cw])
    ubuf_ref[halo:halo + tm, :] = u
    acc = jnp.zeros((tm, cw), F32) + convb_ref[...]
    off = halo - (ksize - 1)
    for j in range(ksize):
        acc = acc + convw_ref[j:j + 1, :] * ubuf_ref[off + j:off + j + tm, :]
    ubuf_ref[0:halo, :] = ubuf_ref[tm:tm + halo, :]
    yc = _ln(acc, LN_EPS) * cg_ref[...] + cb_ref[...]
    yconv_ref[0] = yc * _sigmoid(yc)

    prw = p[:, 2 * cw:]
    rolled = pltpu.roll(prw, 1, 0)
    row = lax.broadcasted_iota(jnp.int32, prw.shape, 0)
    p_prev = jnp.where(row == 0, prev_ref[0:1, :], rolled)
    prev_ref[0:1, :] = prw[tm - 1:tm, :]
    xm = prw + (p_prev - prw) * mu_ref[...]
    r = xm[:, 0:rw]
    k = xm[:, rw:2 * rw]
    v = xm[:, 2 * rw:3 * rw]
    wa = xm[:, 3 * rw:3 * rw + LANES]
    gd = xm[:, 3 * rw + LANES:3 * rw + 2 * LANES]
    lane = lax.broadcasted_iota(jnp.int32, wa.shape, 1)
    wa = jnp.where(lane < LANES // 2, jnp.tanh(wa), wa)
    t12 = _dot(wa, wa2_ref[...], HIGHEST)
    g = _dot(_sigmoid(gd), g2_ref[...], HIGHEST)
    y = w0_ref[...] + t12[:, 0:rw]
    sp = jnp.maximum(-y, 0.0) + jnp.log(1.0 + jnp.exp(-jnp.abs(y)))
    lw = -jnp.exp(-sp - 0.5)
    a = _sigmoid(a0_ref[...] + t12[:, rw:2 * rw])
    kk = k * kkw_ref[...]
    ss = _head_sums(kk * kk, bd_ref[...])
    kk = kk / jnp.maximum(jnp.sqrt(ss), 1e-12)
    kmod = k * (1.0 + (a - 1.0) * kaw_ref[...])
    bonus = _head_sums(r * kmod * rk_ref[...], bd_ref[...]) * v
    rw_ref[0, :, 0:rw] = r
    rw_ref[0, :, rw:2 * rw] = kmod
    rw_ref[0, :, 2 * rw:3 * rw] = v
    rw_ref[0, :, 3 * rw:4 * rw] = kk
    rw_ref[0, :, 4 * rw:5 * rw] = kk * a
    rw_ref[0, :, 5 * rw:6 * rw] = lw
    gb_ref[0, :, 0:rw] = g
    gb_ref[0, :, rw:2 * rw] = bonus


def _front_call(x, mod3, w_in, mu_shift, conv_w, conv_b, cg, cb, w0, a0, wa2, g2, kkw, kaw,
                rk, bd, tm):
    bsz, seq, dm = x.shape
    ksize, cw = conv_w.shape
    rw = w0.shape[-1]
    halo = 32
    assert ksize - 1 <= halo <= tm
    row = lambda a: a.reshape(1, -1)
    consts = [w_in, row(mu_shift), conv_w, row(conv_b), row(cg), row(cb), row(w0), row(a0),
              wa2, g2, row(kkw), row(kaw), row(rk), bd]
    kern = functools.partial(_front_kernel, dm=dm, cw=cw, rw=rw, ksize=ksize, halo=halo)
    return pl.pallas_call(
        kern,
        name="front",
        out_shape=(jax.ShapeDtypeStruct((bsz, seq, 6 * rw), F32),
                   jax.ShapeDtypeStruct((bsz, seq, cw), F32),
                   jax.ShapeDtypeStruct((bsz, seq, 2 * rw), F32)),
        grid=(bsz, seq // tm),
        in_specs=[pl.BlockSpec((1, tm, dm), lambda b, s: (b, s, 0)),
                  pl.BlockSpec((1, 1, mod3.shape[-1]), lambda b, s: (b, 0, 0))]
                 + [_const_spec(a.shape) for a in consts],
        out_specs=(pl.BlockSpec((1, tm, 6 * rw), lambda b, s: (b, s, 0)),
                   pl.BlockSpec((1, tm, cw), lambda b, s: (b, s, 0)),
                   pl.BlockSpec((1, tm, 2 * rw), lambda b, s: (b, s, 0))),
        scratch_shapes=[pltpu.VMEM((tm + halo, cw), F32),
                        pltpu.VMEM((8, w_in.shape[1] - 2 * cw), F32)],
        compiler_params=_cparams(("arbitrary", "arbitrary")),
    )(x, mod3, *consts)


_NN = (((2,), (1,)), ((0,), (0,)))
_NT = (((2,), (2,)), ((0,), (0,)))
_TN = (((1,), (1,)), ((0,), (0,)))


def _split(x):
    hi = x.astype(BF16)
    return hi, (x - hi.astype(F32)).astype(BF16)


def _mm3(a, b, dims):
    (ah, al), (bh, bl) = a, b
    dg = lambda p, q: lax.dot_general(p, q, dims, preferred_element_type=F32)
    return dg(ah, bh) + dg(ah, bl) + dg(al, bh)


def _scan_kernel(rw_ref, tri_ref, y_ref, state_ref, *, nh, hd):
    L = rw_ref.shape[1]
    rwid = nh * hd

    @pl.when(pl.program_id(1) == 0)
    def _():
        state_ref[...] = jnp.zeros(state_ref.shape, F32)

    blk = rw_ref[0]
    tri = tri_ref[...]
    c = _dot(tri, blk[:, 5 * rwid:6 * rwid], HIGHEST)
    lw = blk[:, 5 * rwid:6 * rwid]
    e_pos = jnp.exp(c)
    e_neg = jnp.exp(-c)
    rt_all = blk[:, 0:rwid] * e_pos
    kh_all = blk[:, rwid:2 * rwid] * e_neg
    kt_all = blk[:, 3 * rwid:4 * rwid] * jnp.exp(c - lw)
    bh_all = blk[:, 4 * rwid:5 * rwid] * e_neg
    v_all = blk[:, 2 * rwid:3 * rwid]
    pl_all = e_pos[L - 1:L, :]

    def heads(t):
        return jnp.stack([t[:, h * hd:(h + 1) * hd] for h in range(nh)], axis=0)

    rt, kh, kt, bh, v = heads(rt_all), heads(kh_all), heads(kt_all), heads(bh_all), heads(v_all)
    plh = heads(pl_all)
    s0 = state_ref[...]

    strict = (tri - jnp.eye(L, dtype=F32))[None]
    incl = tri[None]
    kt_s, bh_s, kh_s, rt_s, v_s, s0_s = (_split(t) for t in (kt, bh, kh, rt, v, s0))
    t_bb = _mm3(kt_s, bh_s, _NT) * strict
    t_bk = _mm3(kt_s, kh_s, _NT) * strict
    g_rb = _mm3(rt_s, bh_s, _NT) * incl
    g_rk = _mm3(rt_s, kh_s, _NT) * incl

    w = _mm3(kt_s, s0_s, _NT) + _mm3(_split(t_bk), v_s, _NN)
    tp_s = _split(t_bb)
    w = w - _mm3(tp_s, _split(w), _NN)
    n = 2
    while n < L:
        tp_s = _split(_mm3(tp_s, tp_s, _NN))
        w = w + _mm3(tp_s, _split(w), _NN)
        n *= 2
    u_s = _split(w)
    y = (_mm3(rt_s, s0_s, _NT) - _mm3(_split(g_rb), u_s, _NN)
         + _mm3(_split(g_rk), v_s, _NN))
    state_ref[...] = (s0 * plh + _mm3(v_s, _split(kh * plh), _TN)
                      - _mm3(u_s, _split(bh * plh), _TN))
    y_ref[0] = jnp.concatenate([y[h] for h in range(nh)], axis=-1)


def _scan_call(rwpack, nh, hd, chunk):
    bsz, seq, w6 = rwpack.shape
    rwid = nh * hd
    tri = jnp.tril(jnp.ones((chunk, chunk), F32))
    kern = functools.partial(_scan_kernel, nh=nh, hd=hd)
    return pl.pallas_call(
        kern,
        name="rwkv_scan",
        out_shape=jax.ShapeDtypeStruct((bsz, seq, rwid), F32),
        grid=(bsz, seq // chunk),
        in_specs=[pl.BlockSpec((1, chunk, w6), lambda b, s: (b, s, 0)),
                  _const_spec(tri.shape)],
        out_specs=pl.BlockSpec((1, chunk, rwid), lambda b, s: (b, s, 0)),
        scratch_shapes=[pltpu.VMEM((nh, hd, hd), F32)],
        compiler_params=_cparams(("arbitrary", "arbitrary")),
    )(rwpack, tri)


def _extract_max(s, iota, fill):
    m = jnp.max(s, axis=0, keepdims=True)
    idx = jnp.min(jnp.where(s == m, iota, fill), axis=0, keepdims=True)
    return m, idx, jnp.where(iota == idx, -jnp.inf, s)


def _topk_cols(s, k):
    n = s.shape[0]
    iota = lax.broadcasted_iota(jnp.int32, s.shape, 0)
    vals, idxs = [], []
    for _ in range(k):
        m, idx, s = _extract_max(s, iota, n)
        vals.append(m)
        idxs.append(idx)
    return jnp.concatenate(vals, axis=0), jnp.concatenate(idxs, axis=0)


def _post_kernel(x_ref, y_ref, yconv_ref, gb_ref, mod_ref, wout_ref, lnxg_ref, lnxb_ref,
                 ln1g_ref, ln1b_ref, wq_ref, k1_ref, k2_ref, bd_ref,
                 x1_ref, h2_ref, e_ref, gate_ref, q_ref, *, dm, rw, hd, alpha, nkeys, topk):
    tm = x_ref.shape[1]
    nheads = k1_ref.shape[0]
    half = k1_ref.shape[2]
    x = x_ref[0]
    mod = mod_ref[0]
    gate1 = mod[:, 2 * dm:3 * dm]
    shift2 = mod[:, 3 * dm:4 * dm]
    scale2 = mod[:, 4 * dm:5 * dm]

    y = y_ref[0]
    bd = bd_ref[...]
    mu = _head_sums(y, bd) * (1.0 / hd)
    yc = y - mu
    var = _head_sums(yc * yc, bd) * (1.0 / hd)
    gn = yc * lax.rsqrt(var + GN_EPS)
    gb = gb_ref[0]
    y_rw = (gn * lnxg_ref[...] + lnxb_ref[...] + gb[:, rw:2 * rw]) * gb[:, 0:rw]
    ycat = jnp.concatenate([yconv_ref[0], y_rw], axis=-1)
    y1 = _dot(ycat.astype(BF16), wout_ref[...])
    x1 = _ln(alpha * x + gate1 * y1, LN_EPS) * ln1g_ref[...] + ln1b_ref[...]
    x1_ref[0] = x1
    h2 = _ln(x1, LN_EPS) * (1.0 + scale2) + shift2
    h2_ref[0] = h2
    q = _dot(h2.astype(BF16), wq_ref[...])
    for i in range(2 * nheads):
        q_ref[i] = q[:, i * half:(i + 1) * half]

    sub = 8
    assert topk == 2 * sub
    r16 = lax.broadcasted_iota(jnp.int32, (topk, LANES), 0)
    r8 = lax.broadcasted_iota(jnp.int32, (sub, LANES), 0)
    flat = jnp.concatenate([r16] + [a * topk + r8 for a in range(1, sub)] + [(sub + r8) * topk],
                           axis=0)
    ncand = topk * topk

    def route(h, c0):
        s1 = lax.dot_general(k1_ref[h], q_ref[2 * h, c0:c0 + LANES, :], (((1,), (1,)), ((), ())),
                             preferred_element_type=F32, precision=HIGHEST)
        s2 = lax.dot_general(k2_ref[h], q_ref[2 * h + 1, c0:c0 + LANES, :],
                             (((1,), (1,)), ((), ())),
                             preferred_element_type=F32, precision=HIGHEST)
        v1, i1 = _topk_cols(s1, topk)
        v2, i2 = _topk_cols(s2, topk)
        cand = jnp.concatenate([v1[0:1, :] + v2]
                               + [v1[a:a + 1, :] + v2[0:sub, :] for a in range(1, sub)]
                               + [v1[sub:, :] + v2[0:1, :]], axis=0)
        eall = jnp.concatenate([i1[0:1, :] * nkeys + i2]
                               + [i1[a:a + 1, :] * nkeys + i2[0:sub, :] for a in range(1, sub)]
                               + [i1[sub:, :] * nkeys + i2[0:1, :]], axis=0)
        scs, exs = [], []
        for _ in range(topk):
            m, idx, cand_next = _extract_max(cand, flat, ncand)
            exs.append(jnp.max(jnp.where(flat == idx, eall, -1), axis=0, keepdims=True))
            scs.append(m)
            cand = cand_next
        sc = jnp.concatenate(scs, axis=0)
        ex = jnp.concatenate(exs, axis=0)
        pexp = jnp.exp(sc - sc[0:1, :])
        gates = pexp / jnp.sum(pexp, axis=0, keepdims=True)
        base = pl.multiple_of(h * topk, topk)
        e_ref[pl.ds(base, topk), c0:c0 + LANES] = ex
        gate_ref[pl.ds(base, topk), c0:c0 + LANES] = gates

    def head_body(h, carry):
        for c0 in range(0, tm, LANES):
            route(h, c0)
        return carry

    lax.fori_loop(0, nheads, head_body, 0)


def _post_call(x, yscan, yconv, gb, mod3, w_out, lnxg, lnxb, ln1g, ln1b, wq, k1, k2, bd,
               hd, alpha, topk, tm):
    bsz, seq, dm = x.shape
    rw = yscan.shape[-1]
    cw = yconv.shape[-1]
    nheads, nkeys, half = k1.shape
    nslot = nheads * topk
    ntok = bsz * seq
    nst = seq // tm
    row = lambda a: a.reshape(1, -1)
    consts = [w_out, row(lnxg), row(lnxb), row(ln1g), row(ln1b), wq, k1, k2, bd]
    kern = functools.partial(_post_kernel, dm=dm, rw=rw, hd=hd, alpha=alpha, nkeys=nkeys,
                             topk=topk)
    tok = lambda w: pl.BlockSpec((1, tm, w), lambda b, s: (b, s, 0))
    slot = pl.BlockSpec((nslot, tm), lambda b, s: (0, b * nst + s))
    return pl.pallas_call(
        kern,
        name="post_route",
        out_shape=(jax.ShapeDtypeStruct((bsz, seq, dm), F32),
                   jax.ShapeDtypeStruct((bsz, seq, dm), F32),
                   jax.ShapeDtypeStruct((nslot, ntok), jnp.int32),
                   jax.ShapeDtypeStruct((nslot, ntok), F32)),
        grid=(bsz, nst),
        in_specs=[tok(dm), tok(rw), tok(cw), tok(2 * rw),
                  pl.BlockSpec((1, 1, mod3.shape[-1]), lambda b, s: (b, 0, 0))]
                 + [_const_spec(a.shape) for a in consts],
        out_specs=(tok(dm), tok(dm), slot, slot),
        scratch_shapes=[pltpu.VMEM((2 * nheads, tm, half), F32)],
        compiler_params=_cparams(("arbitrary", "arbitrary")),
    )(x, yscan, yconv, gb, mod3, *consts)


def _pack_kernel(t_ref, o_ref):
    t = t_ref[...]
    hw = t.shape[1] // 2
    hi = lax.bitcast_convert_type(t[:, 0:hw].astype(BF16).astype(F32), jnp.uint32)
    lo = lax.bitcast_convert_type(t[:, hw:].astype(BF16).astype(F32), jnp.uint32)
    o_ref[...] = (hi & jnp.uint32(0xFFFF0000)) | (lo >> 16)


def _pack_call(table):
    ne, dm = table.shape
    packed = pl.pallas_call(
        _pack_kernel,
        name="peer_pack",
        out_shape=jax.ShapeDtypeStruct((ne, dm // 2), jnp.uint32),
        grid=(ne // PACK_ROWS,),
        in_specs=[pl.BlockSpec((PACK_ROWS, dm), lambda i: (i, 0))],
        out_specs=pl.BlockSpec((PACK_ROWS, dm // 2), lambda i: (i, 0)),
        compiler_params=_cparams(("arbitrary",)),
    )(table)
    return packed.reshape(ne * (dm // (2 * LANES)), LANES)


def _unpack(w):
    hi = lax.bitcast_convert_type(w & jnp.uint32(0xFFFF0000), F32)
    lo = lax.bitcast_convert_type(w << 16, F32)
    return hi, lo


def _gather_pair(tbl_ref, e_ref, off, nj):
    rows = [tbl_ref[pl.ds(pl.multiple_of(e_ref[off + d], nj), nj), :] for d in range(2)]
    return jnp.concatenate(rows, axis=0)


def _peer_u_kernel(e_ref, h_ref, tbl_ref, z_ref, p0_ref, p1_ref, *, nslot, nj):
    tm = h_ref.shape[0]
    rows = 2 * nj
    lane = lax.broadcasted_iota(jnp.int32, (rows, LANES), 1)
    z_ref[...] = jnp.zeros(z_ref.shape, F32)

    @pl.when(pl.program_id(0) == 0)
    def _():
        p1_ref[...] = jnp.zeros(p1_ref.shape, F32)

    def reduce_rows(p_ref, row, tcol):
        rs = jnp.sum(p_ref[row:row + rows, :], axis=1, keepdims=True)
        z_ref[row:row + rows, :] = jnp.where(lane == tcol, rs, z_ref[row:row + rows, :])

    def one_token(t, p_ref, q_ref):
        hv = h_ref[t]
        h_hi = jnp.concatenate([hv[0:nj], hv[0:nj]], axis=0)
        h_lo = jnp.concatenate([hv[nj:2 * nj], hv[nj:2 * nj]], axis=0)
        ev = e_ref.at[pl.ds(t * nslot, nslot)]
        for i in range(0, nslot, 2):
            hi, lo = _unpack(_gather_pair(tbl_ref, ev, i, nj))
            p_ref[i * nj:i * nj + rows, :] = hi * h_hi + lo * h_lo
            reduce_rows(q_ref, i * nj, t - 1)

    def tok_body(i, carry):
        one_token(2 * i, p0_ref, p1_ref)
        one_token(2 * i + 1, p1_ref, p0_ref)
        return carry

    lax.fori_loop(0, tm // 2, tok_body, 0)
    for r in range(0, nslot * nj, rows):
        reduce_rows(p1_ref, r, tm - 1)


def _peer_u_call(experts, h3, tbl, tm):
    ntok = h3.shape[0]
    nslot = experts.shape[0] // ntok
    nj = h3.shape[1] // 2
    kern = functools.partial(_peer_u_kernel, nslot=nslot, nj=nj)
    return pl.pallas_call(
        kern,
        name="peer_u",
        out_shape=jax.ShapeDtypeStruct((nslot * nj, ntok), F32),
        grid=(ntok // tm,),
        in_specs=[pl.BlockSpec((tm * nslot,), lambda i: (i,), memory_space=pltpu.SMEM),
                  pl.BlockSpec((tm, 2 * nj, LANES), lambda i: (i, 0, 0)),
                  _const_spec(tbl.shape)],
        out_specs=pl.BlockSpec((nslot * nj, tm), lambda i: (0, i)),
        scratch_shapes=[pltpu.VMEM((nslot * nj, LANES), F32),
                        pltpu.VMEM((nslot * nj, LANES), F32)],
        compiler_params=_cparams(("arbitrary",)),
    )(experts, h3, tbl)


def _act_kernel(z_ref, sel_ref, gate_ref, a_ref):
    z = _dot(sel_ref[...], z_ref[...], HIGHEST)
    a = 0.5 * z * (1.0 + lax.erf(z * (1.0 / math.sqrt(2.0)))) * gate_ref[...]
    a_ref[...] = a.T


def _act_call(zpart, gates, tm):
    nslot, ntok = gates.shape
    nj = zpart.shape[0] // nslot
    sel = (jnp.arange(nslot * nj)[None, :] // nj == jnp.arange(nslot)[:, None]).astype(F32)
    return pl.pallas_call(
        _act_kernel,
        name="peer_act",
        out_shape=jax.ShapeDtypeStruct((ntok, nslot), F32),
        grid=(ntok // tm,),
        in_specs=[pl.BlockSpec((nslot * nj, tm), lambda i: (0, i)),
                  _const_spec(sel.shape),
                  pl.BlockSpec((nslot, tm), lambda i: (0, i))],
        out_specs=pl.BlockSpec((tm, nslot), lambda i: (i, 0)),
        compiler_params=_cparams(("arbitrary",)),
    )(zpart, sel, gates)


def _peer_v_kernel(e_ref, act_ref, tbl_ref, y_ref, w0_ref, w1_ref, *, nslot, nj, nacc):
    tm = y_ref.shape[0]
    rows = 2 * nj
    upper = lax.broadcasted_iota(jnp.int32, (rows, LANES), 0) >= nj

    def expand(t, w_ref):
        w_ref[...] = jnp.broadcast_to(act_ref[pl.ds(t, 1), :], (nslot, nslot)).T

    def one_token(t, w_ref, wn_ref):
        expand(jnp.minimum(t + 1, tm - 1), wn_ref)
        ev = e_ref.at[pl.ds(t * nslot, nslot)]
        accs = [jnp.zeros((rows, LANES), F32) for _ in range(2 * nacc)]
        for i in range(0, nslot, 2):
            hi, lo = _unpack(_gather_pair(tbl_ref, ev, i, nj))
            a = jnp.where(upper, w_ref[i + 1:i + 2, :], w_ref[i:i + 1, :])
            j = (i // 2) % nacc
            accs[2 * j] = accs[2 * j] + a * hi
            accs[2 * j + 1] = accs[2 * j + 1] + a * lo
        acc_hi = sum(accs[2::2], accs[0])
        acc_lo = sum(accs[3::2], accs[1])
        y_ref[t] = jnp.concatenate([acc_hi[0:nj] + acc_hi[nj:], acc_lo[0:nj] + acc_lo[nj:]],
                                   axis=0)

    expand(0, w0_ref)

    def tok_body(i, carry):
        one_token(2 * i, w0_ref, w1_ref)
        one_token(2 * i + 1, w1_ref, w0_ref)
        return carry

    lax.fori_loop(0, tm // 2, tok_body, 0)


def _peer_v_call(experts, act, tbl, nj, tm):
    ntok, nslot = act.shape
    assert nslot == LANES
    kern = functools.partial(_peer_v_kernel, nslot=nslot, nj=nj, nacc=2)
    return pl.pallas_call(
        kern,
        name="peer_v",
        out_shape=jax.ShapeDtypeStruct((ntok, 2 * nj, LANES), F32),
        grid=(ntok // tm,),
        in_specs=[pl.BlockSpec((tm * nslot,), lambda i: (i,), memory_space=pltpu.SMEM),
                  pl.BlockSpec((tm, nslot), lambda i: (i, 0)),
                  _const_spec(tbl.shape)],
        out_specs=pl.BlockSpec((tm, 2 * nj, LANES), lambda i: (i, 0, 0)),
        scratch_shapes=[pltpu.VMEM((nslot, LANES), F32), pltpu.VMEM((nslot, LANES), F32)],
        compiler_params=_cparams(("arbitrary",)),
    )(experts, act, tbl)


def _final_kernel(x_ref, y_ref, mod_ref, g_ref, b_ref, o_ref, *, dm, alpha):
    gate2 = mod_ref[0][:, 5 * dm:6 * dm]
    o_ref[0] = _ln(alpha * x_ref[0] + gate2 * y_ref[0], LN_EPS) * g_ref[...] + b_ref[...]


def _final_call(x1, y2, mod3, g, b, alpha, tm):
    bsz, seq, dm = x1.shape
    tok = pl.BlockSpec((1, tm, dm), lambda bi, s: (bi, s, 0))
    return pl.pallas_call(
        functools.partial(_final_kernel, dm=dm, alpha=alpha),
        name="final_ln",
        out_shape=jax.ShapeDtypeStruct((bsz, seq, dm), F32),
        grid=(bsz, seq // tm),
        in_specs=[tok, tok, pl.BlockSpec((1, 1, mod3.shape[-1]), lambda bi, s: (bi, 0, 0)),
                  _const_spec((1, dm)), _const_spec((1, dm))],
        out_specs=tok,
        compiler_params=_cparams(("arbitrary", "arbitrary")),
    )(x1, y2, mod3, g.reshape(1, -1), b.reshape(1, -1))


def kernel(x, c, cond_w, cond_b, w_in, mu_shift, conv_w, conv_b, conv_ln_g, conv_ln_b, rw_w0, rw_w2, rw_a0, rw_a2, rw_g2, rw_kk, rw_ka, rw_rk, rw_lnx_g, rw_lnx_b, w_out, ln1_g, ln1_b, peer_wq, peer_k1, peer_k2, peer_u, peer_v, ln2_g, ln2_b):
    bsz, seq, dm = x.shape
    nh, hd = rw_rk.shape
    rw = nh * hd
    lora_w, lora_a = rw_w2.shape[0], rw_a2.shape[0]
    assert lora_w == lora_a == LANES // 2 and rw_g2.shape[0] == LANES
    topk = PEER_TOPK
    alpha = (2.0 * DEPTH) ** 0.25
    tm = min(ROW_TILE, seq)
    chunk = min(SCAN_CHUNK, seq)

    mod3 = _mod_call(c, cond_w, cond_b).reshape(bsz, 1, -1)

    head_id = jnp.arange(rw) // hd
    bd = (head_id[:, None] == head_id[None, :]).astype(BF16)
    wa2 = jnp.zeros((LANES, 2 * rw), F32)
    wa2 = wa2.at[0:lora_w, 0:rw].set(rw_w2).at[lora_w:, rw:].set(rw_a2)

    rwpack, yconv, gb = _front_call(
        x, mod3, w_in.astype(BF16), mu_shift, conv_w, conv_b, conv_ln_g, conv_ln_b, rw_w0,
        rw_a0, wa2, rw_g2, rw_kk, rw_ka, rw_rk.reshape(-1), bd, tm)
    yscan = _scan_call(rwpack, nh, hd, chunk)
    x1, h2, experts, gates = _post_call(
        x, yscan, yconv, gb, mod3, w_out.astype(BF16), rw_lnx_g, rw_lnx_b, ln1_g, ln1_b,
        peer_wq.astype(BF16), peer_k1, peer_k2, bd, hd, alpha, topk, tm)

    ntok = bsz * seq
    ptm = min(PEER_TOK, ntok)
    h3 = h2.reshape(ntok, dm // LANES, LANES)
    nj = dm // (2 * LANES)
    experts_t = (experts * nj).T.reshape(-1)
    zpart = _peer_u_call(experts_t, h3, _pack_call(peer_u), ptm)
    act = _act_call(zpart, gates, ptm)
    y3 = _peer_v_call(experts_t, act, _pack_call(peer_v), nj, ptm)
    y2 = y3.reshape(bsz, seq, dm)
    return _final_call(x1, y2, mod3, ln2_g, ln2_b, alpha, tm)
```

```python
import functools
import math

import jax
import jax.numpy as jnp
from jax import lax
from jax.experimental import pallas as pl
from jax.experimental.pallas import tpu as pltpu

F32 = jnp.float32
BF16 = jnp.bfloat16
HIGHEST = lax.Precision.HIGHEST

LN_EPS = 1e-5
GN_EPS = 64e-5
PEER_TOPK = 16
DEPTH = 1
LANES = 128
ROW_TILE = 256
FRONT_TILE = 256
SCAN_CHUNK = 64
PEER_TOK = 128
U_TOKENS_PER_TRIP = 8
V_TOKENS_PER_TRIP = 2
PACK_ROWS = 512
VMEM_LIMIT = 56 * 1024 * 1024


def _cparams(sem):
    return pltpu.CompilerParams(dimension_semantics=sem, vmem_limit_bytes=VMEM_LIMIT)


def _dot(a, b, precision=None):
    return jnp.dot(a, b, preferred_element_type=F32, precision=precision)


def _sigmoid(x):
    return 1.0 / (1.0 + jnp.exp(-x))


def _ln(x, eps):
    mu = jnp.mean(x, axis=-1, keepdims=True)
    xc = x - mu
    var = jnp.mean(xc * xc, axis=-1, keepdims=True)
    return xc * lax.rsqrt(var + eps)


def _head_sums(x, bd):
    hi = x.astype(BF16)
    lo = (x - hi.astype(F32)).astype(BF16)
    return _dot(hi, bd) + _dot(lo, bd)


def _const_spec(shape):
    nd = len(shape)
    return pl.BlockSpec(shape, lambda *_: (0,) * nd, pipeline_mode=pl.Buffered(1))


def _mod_kernel(c_ref, w_ref, b_ref, o_ref):
    c = c_ref[...]
    o_ref[...] = _dot(c * _sigmoid(c), w_ref[...], HIGHEST) + b_ref[...]


def _mod_call(c, cond_w, cond_b):
    bsz, dm = c.shape
    nblk = cond_w.shape[1] // dm
    return pl.pallas_call(
        _mod_kernel,
        name="mod",
        out_shape=jax.ShapeDtypeStruct((bsz, nblk * dm), F32),
        grid=(nblk,),
        in_specs=[pl.BlockSpec((bsz, dm), lambda j: (0, 0)),
                  pl.BlockSpec((dm, dm), lambda j: (0, j)),
                  pl.BlockSpec((1, dm), lambda j: (0, j))],
        out_specs=pl.BlockSpec((bsz, dm), lambda j: (0, j)),
        compiler_params=_cparams(("arbitrary",)),
    )(c, cond_w, cond_b.reshape(1, -1))


def _front_kernel(x_ref, mod_ref, win_ref, mu_ref, convw_ref, convb_ref, cg_ref, cb_ref,
                  w0_ref, a0_ref, wa2_ref, g2_ref, kkw_ref, kaw_ref, rk_ref, bd_ref,
                  rw_ref, yconv_ref, gb_ref, ubuf_ref, prev_ref, *, dm, cw, rw, ksize, halo):
    tm = x_ref.shape[1]
    first = pl.program_id(1) == 0

    @pl.when(first)
    def _():
        ubuf_ref[0:halo, :] = jnp.zeros((halo, cw), F32)
        prev_ref[...] = jnp.zeros(prev_ref.shape, F32)

    x = x_ref[0]
    mod = mod_ref[0]
    shift1 = mod[:, 0:dm]
    scale1 = mod[:, dm:2 * dm]
    h = _ln(x, LN_EPS) * (1.0 + scale1) + shift1
    p = _dot(h.astype(BF16), win_ref[...])

    u = p[:, 0:cw] * _sigmoid(p[:, cw:2 * cw])
    ubuf_ref[halo:halo + tm, :] = u
    acc = jnp.zeros((tm, cw), F32) + convb_ref[...]
    off = halo - (ksize - 1)
    for j in range(ksize):
        acc = acc + convw_ref[j:j + 1, :] * ubuf_ref[off + j:off + j + tm, :]
    ubuf_ref[0:halo, :] = ubuf_ref[tm:tm + halo, :]
    yc = _ln(acc, LN_EPS) * cg_ref[...] + cb_ref[...]
    yconv_ref[0] = yc * _sigmoid(yc)

    prw = p[:, 2 * cw:]
    rolled = pltpu.roll(prw, 1, 0)
    row = lax.broadcasted_iota(jnp.int32, prw.shape, 0)
    p_prev = jnp.where(row == 0, prev_ref[0:1, :], rolled)
    prev_ref[0:1, :] = prw[tm - 1:tm, :]
    xm = prw + (p_prev - prw) * mu_ref[...]
    r = xm[:, 0:rw]
    k = xm[:, rw:2 * rw]
    v = xm[:, 2 * rw:3 * rw]
    wa = xm[:, 3 * rw:3 * rw + LANES]
    gd = xm[:, 3 * rw + LANES:3 * rw + 2 * LANES]
    lane = lax.broadcasted_iota(jnp.int32, wa.shape, 1)
    wa = jnp.where(lane < LANES // 2, jnp.tanh(wa), wa)
    t12 = _dot(wa, wa2_ref[...], HIGHEST)
    g = _dot(_sigmoid(gd), g2_ref[...], HIGHEST)
    y = w0_ref[...] + t12[:, 0:rw]
    sp = jnp.maximum(-y, 0.0) + jnp.log(1.0 + jnp.exp(-jnp.abs(y)))
    lw = -jnp.exp(-sp - 0.5)
    a = _sigmoid(a0_ref[...] + t12[:, rw:2 * rw])
    kk = k * kkw_ref[...]
    ss = _head_sums(kk * kk, bd_ref[...])
    kk = kk / jnp.maximum(jnp.sqrt(ss), 1e-12)
    kmod = k * (1.0 + (a - 1.0) * kaw_ref[...])
    bonus = _head_sums(r * kmod * rk_ref[...], bd_ref[...]) * v
    rw_ref[0, :, 0:rw] = r
    rw_ref[0, :, rw:2 * rw] = kmod
    rw_ref[0, :, 2 * rw:3 * rw] = v
    rw_ref[0, :, 3 * rw:4 * rw] = kk
    rw_ref[0, :, 4 * rw:5 * rw] = kk * a
    rw_ref[0, :, 5 * rw:6 * rw] = lw
    gb_ref[0, :, 0:rw] = g
    gb_ref[0, :, rw:2 * rw] = bonus


def _front_call(x, mod3, w_in, mu_shift, conv_w, conv_b, cg, cb, w0, a0, wa2, g2, kkw, kaw,
                rk, bd, tm):
    bsz, seq, dm = x.shape
    ksize, cw = conv_w.shape
    rw = w0.shape[-1]
    halo = 32
    assert ksize - 1 <= halo <= tm
    row = lambda a: a.reshape(1, -1)
    consts = [w_in, row(mu_shift), conv_w, row(conv_b), row(cg), row(cb), row(w0), row(a0),
              wa2, g2, row(kkw), row(kaw), row(rk), bd]
    kern = functools.partial(_front_kernel, dm=dm, cw=cw, rw=rw, ksize=ksize, halo=halo)
    return pl.pallas_call(
        kern,
        name="front",
        out_shape=(jax.ShapeDtypeStruct((bsz, seq, 6 * rw), F32),
                   jax.ShapeDtypeStruct((bsz, seq, cw), F32),
                   jax.ShapeDtypeStruct((bsz, seq, 2 * rw), F32)),
        grid=(bsz, seq // tm),
        in_specs=[pl.BlockSpec((1, tm, dm), lambda b, s: (b, s, 0)),
                  pl.BlockSpec((1, 1, mod3.shape[-1]), lambda b, s: (b, 0, 0))]
                 + [_const_spec(a.shape) for a in consts],
        out_specs=(pl.BlockSpec((1, tm, 6 * rw), lambda b, s: (b, s, 0)),
                   pl.BlockSpec((1, tm, cw), lambda b, s: (b, s, 0)),
                   pl.BlockSpec((1, tm, 2 * rw), lambda b, s: (b, s, 0))),
        scratch_shapes=[pltpu.VMEM((tm + halo, cw), F32),
                        pltpu.VMEM((8, w_in.shape[1] - 2 * cw), F32)],
        compiler_params=_cparams(("arbitrary", "arbitrary")),
    )(x, mod3, *consts)


SCAN_PASSES = 1
_NN = (((2,), (1,)), ((0,), (0,)))
_NT = (((2,), (2,)), ((0,), (0,)))
_TN = (((1,), (1,)), ((0,), (0,)))


def _split(x, passes=3):
    hi = x.astype(BF16)
    if passes == 1:
        return hi, None
    return hi, (x - hi.astype(F32)).astype(BF16)


def _mm3(a, b, dims):
    (ah, al), (bh, bl) = a, b
    dg = lambda p, q: lax.dot_general(p, q, dims, preferred_element_type=F32)
    out = dg(ah, bh)
    if bl is not None:
        out = out + dg(ah, bl)
    if al is not None:
        out = out + dg(al, bh)
    return out


def _scan_kernel(rw_ref, tri_ref, y_ref, state_ref, *, nh, hd):
    L = rw_ref.shape[1]
    rwid = nh * hd

    @pl.when(pl.program_id(1) == 0)
    def _():
        state_ref[...] = jnp.zeros(state_ref.shape, F32)

    blk = rw_ref[0]
    tri = tri_ref[...]
    c = _dot(tri, blk[:, 5 * rwid:6 * rwid], HIGHEST)
    lw = blk[:, 5 * rwid:6 * rwid]
    e_pos = jnp.exp(c)
    e_neg = jnp.exp(-c)
    rt_all = blk[:, 0:rwid] * e_pos
    kh_all = blk[:, rwid:2 * rwid] * e_neg
    kt_all = blk[:, 3 * rwid:4 * rwid] * jnp.exp(c - lw)
    bh_all = blk[:, 4 * rwid:5 * rwid] * e_neg
    v_all = blk[:, 2 * rwid:3 * rwid]
    pl_all = e_pos[L - 1:L, :]

    def heads(t):
        return jnp.stack([t[:, h * hd:(h + 1) * hd] for h in range(nh)], axis=0)

    rt, kh, kt, bh, v = heads(rt_all), heads(kh_all), heads(kt_all), heads(bh_all), heads(v_all)
    plh = heads(pl_all)
    s0 = state_ref[...]

    strict = (tri - jnp.eye(L, dtype=F32))[None]
    incl = tri[None]
    sp = functools.partial(_split, passes=SCAN_PASSES)
    kt_s, bh_s, kh_s, rt_s, v_s, s0_s = (sp(t) for t in (kt, bh, kh, rt, v, s0))
    t_bb = _mm3(kt_s, bh_s, _NT) * strict
    t_bk = _mm3(kt_s, kh_s, _NT) * strict
    g_rb = _mm3(rt_s, bh_s, _NT) * incl
    g_rk = _mm3(rt_s, kh_s, _NT) * incl

    w = _mm3(kt_s, s0_s, _NT) + _mm3(sp(t_bk), v_s, _NN)
    tp_s = sp(t_bb)
    w = w - _mm3(tp_s, sp(w), _NN)
    n = 2
    while n < L:
        tp_s = sp(_mm3(tp_s, tp_s, _NN))
        w = w + _mm3(tp_s, sp(w), _NN)
        n *= 2
    u_s = sp(w)
    y = (_mm3(rt_s, s0_s, _NT) - _mm3(sp(g_rb), u_s, _NN)
         + _mm3(sp(g_rk), v_s, _NN))
    state_ref[...] = (s0 * plh + _mm3(v_s, sp(kh * plh), _TN)
                      - _mm3(u_s, sp(bh * plh), _TN))
    y_ref[0] = jnp.concatenate([y[h] for h in range(nh)], axis=-1)


def _scan_call(rwpack, nh, hd, chunk):
    bsz, seq, w6 = rwpack.shape
    rwid = nh * hd
    tri = jnp.tril(jnp.ones((chunk, chunk), F32))
    kern = functools.partial(_scan_kernel, nh=nh, hd=hd)
    return pl.pallas_call(
        kern,
        name="rwkv_scan",
        out_shape=jax.ShapeDtypeStruct((bsz, seq, rwid), F32),
        grid=(bsz, seq // chunk),
        in_specs=[pl.BlockSpec((1, chunk, w6), lambda b, s: (b, s, 0)),
                  _const_spec(tri.shape)],
        out_specs=pl.BlockSpec((1, chunk, rwid), lambda b, s: (b, s, 0)),
        scratch_shapes=[pltpu.VMEM((nh, hd, hd), F32)],
        compiler_params=_cparams(("arbitrary", "arbitrary")),
    )(rwpack, tri)


def _extract_max(s, iota, fill):
    m = jnp.max(s, axis=0, keepdims=True)
    idx = jnp.min(jnp.where(s == m, iota, fill), axis=0, keepdims=True)
    return m, idx, jnp.where(iota == idx, -jnp.inf, s)


def _topk_cols(s, k):
    n = s.shape[0]
    iota = lax.broadcasted_iota(jnp.int32, s.shape, 0)
    vals, idxs = [], []
    for _ in range(k):
        m, idx, s = _extract_max(s, iota, n)
        vals.append(m)
        idxs.append(idx)
    return jnp.concatenate(vals, axis=0), jnp.concatenate(idxs, axis=0)


def _post_kernel(x_ref, y_ref, yconv_ref, gb_ref, mod_ref, wout_ref, lnxg_ref, lnxb_ref,
                 ln1g_ref, ln1b_ref, wq_ref, k1_ref, k2_ref, bd_ref,
                 x1_ref, h2_ref, e_ref, gate_ref, q_ref, *, dm, rw, hd, alpha, nkeys, topk):
    tm = x_ref.shape[1]
    nheads = k1_ref.shape[0]
    half = k1_ref.shape[2]
    x = x_ref[0]
    mod = mod_ref[0]
    gate1 = mod[:, 2 * dm:3 * dm]
    shift2 = mod[:, 3 * dm:4 * dm]
    scale2 = mod[:, 4 * dm:5 * dm]

    y = y_ref[0]
    bd = bd_ref[...]
    mu = _head_sums(y, bd) * (1.0 / hd)
    yc = y - mu
    var = _head_sums(yc * yc, bd) * (1.0 / hd)
    gn = yc * lax.rsqrt(var + GN_EPS)
    gb = gb_ref[0]
    y_rw = (gn * lnxg_ref[...] + lnxb_ref[...] + gb[:, rw:2 * rw]) * gb[:, 0:rw]
    ycat = jnp.concatenate([yconv_ref[0], y_rw], axis=-1)
    y1 = _dot(ycat.astype(BF16), wout_ref[...])
    x1 = _ln(alpha * x + gate1 * y1, LN_EPS) * ln1g_ref[...] + ln1b_ref[...]
    x1_ref[0] = x1
    h2 = _ln(x1, LN_EPS) * (1.0 + scale2) + shift2
    h2_ref[0] = h2
    q = _dot(h2.astype(BF16), wq_ref[...])
    for i in range(2 * nheads):
        q_ref[i] = q[:, i * half:(i + 1) * half]

    sub = 8
    assert topk == 2 * sub
    r16 = lax.broadcasted_iota(jnp.int32, (topk, LANES), 0)
    r8 = lax.broadcasted_iota(jnp.int32, (sub, LANES), 0)
    flat = jnp.concatenate([r16] + [a * topk + r8 for a in range(1, sub)] + [(sub + r8) * topk],
                           axis=0)
    ncand = topk * topk

    def route(h, c0):
        s1 = lax.dot_general(k1_ref[h], q_ref[2 * h, c0:c0 + LANES, :], (((1,), (1,)), ((), ())),
                             preferred_element_type=F32, precision=HIGHEST)
        s2 = lax.dot_general(k2_ref[h], q_ref[2 * h + 1, c0:c0 + LANES, :],
                             (((1,), (1,)), ((), ())),
                             preferred_element_type=F32, precision=HIGHEST)
        v1, i1 = _topk_cols(s1, topk)
        v2, i2 = _topk_cols(s2, topk)
        cand = jnp.concatenate([v1[0:1, :] + v2]
                               + [v1[a:a + 1, :] + v2[0:sub, :] for a in range(1, sub)]
                               + [v1[sub:, :] + v2[0:1, :]], axis=0)
        eall = jnp.concatenate([i1[0:1, :] * nkeys + i2]
                               + [i1[a:a + 1, :] * nkeys + i2[0:sub, :] for a in range(1, sub)]
                               + [i1[sub:, :] * nkeys + i2[0:1, :]], axis=0)
        scs, exs = [], []
        for _ in range(topk):
            m, idx, cand_next = _extract_max(cand, flat, ncand)
            exs.append(jnp.max(jnp.where(flat == idx, eall, -1), axis=0, keepdims=True))
            scs.append(m)
            cand = cand_next
        sc = jnp.concatenate(scs, axis=0)
        ex = jnp.concatenate(exs, axis=0)
        pexp = jnp.exp(sc - sc[0:1, :])
        gates = pexp / jnp.sum(pexp, axis=0, keepdims=True)
        base = pl.multiple_of(h * topk, topk)
        e_ref[pl.ds(base, topk), c0:c0 + LANES] = ex
        gate_ref[pl.ds(base, topk), c0:c0 + LANES] = gates

    def head_body(h, carry):
        for c0 in range(0, tm, LANES):
            route(h, c0)
        return carry

    lax.fori_loop(0, nheads, head_body, 0)


def _post_call(x, yscan, yconv, gb, mod3, w_out, lnxg, lnxb, ln1g, ln1b, wq, k1, k2, bd,
               hd, alpha, topk, tm):
    bsz, seq, dm = x.shape
    rw = yscan.shape[-1]
    cw = yconv.shape[-1]
    nheads, nkeys, half = k1.shape
    nslot = nheads * topk
    ntok = bsz * seq
    nst = seq // tm
    row = lambda a: a.reshape(1, -1)
    consts = [w_out, row(lnxg), row(lnxb), row(ln1g), row(ln1b), wq, k1, k2, bd]
    kern = functools.partial(_post_kernel, dm=dm, rw=rw, hd=hd, alpha=alpha, nkeys=nkeys,
                             topk=topk)
    tok = lambda w: pl.BlockSpec((1, tm, w), lambda b, s: (b, s, 0))
    slot = pl.BlockSpec((nslot, tm), lambda b, s: (0, b * nst + s))
    return pl.pallas_call(
        kern,
        name="post_route",
        out_shape=(jax.ShapeDtypeStruct((bsz, seq, dm), F32),
                   jax.ShapeDtypeStruct((bsz, seq, dm), F32),
                   jax.ShapeDtypeStruct((nslot, ntok), jnp.int32),
                   jax.ShapeDtypeStruct((nslot, ntok), F32)),
        grid=(bsz, nst),
        in_specs=[tok(dm), tok(rw), tok(cw), tok(2 * rw),
                  pl.BlockSpec((1, 1, mod3.shape[-1]), lambda b, s: (b, 0, 0))]
                 + [_const_spec(a.shape) for a in consts],
        out_specs=(tok(dm), tok(dm), slot, slot),
        scratch_shapes=[pltpu.VMEM((2 * nheads, tm, half), F32)],
        compiler_params=_cparams(("arbitrary", "arbitrary")),
    )(x, yscan, yconv, gb, mod3, *consts)


def _pack_kernel(t_ref, o_ref):
    t = t_ref[...]
    hw = t.shape[1] // 2
    hi = lax.bitcast_convert_type(t[:, 0:hw].astype(BF16).astype(F32), jnp.uint32)
    lo = lax.bitcast_convert_type(t[:, hw:].astype(BF16).astype(F32), jnp.uint32)
    o_ref[...] = (hi & jnp.uint32(0xFFFF0000)) | (lo >> 16)


def _pack_call(table):
    ne, dm = table.shape
    packed = pl.pallas_call(
        _pack_kernel,
        name="peer_pack",
        out_shape=jax.ShapeDtypeStruct((ne, dm // 2), jnp.uint32),
        grid=(ne // PACK_ROWS,),
        in_specs=[pl.BlockSpec((PACK_ROWS, dm), lambda i: (i, 0))],
        out_specs=pl.BlockSpec((PACK_ROWS, dm // 2), lambda i: (i, 0)),
        compiler_params=_cparams(("arbitrary",)),
    )(table)
    return packed.reshape(ne * (dm // (2 * LANES)), LANES)


def _unpack(w):
    hi = lax.bitcast_convert_type(w & jnp.uint32(0xFFFF0000), F32)
    lo = lax.bitcast_convert_type(w << 16, F32)
    return hi, lo


def _gather_pair(tbl_ref, e_ref, off, nj):
    rows = [tbl_ref[pl.ds(pl.multiple_of(e_ref[off + d], nj), nj), :] for d in range(2)]
    return jnp.concatenate(rows, axis=0)


def _peer_u_kernel(e_ref, h_ref, tbl_ref, z_ref, p0_ref, p1_ref, *, nslot, nj):
    tm = h_ref.shape[0]
    rows = 2 * nj
    lane = lax.broadcasted_iota(jnp.int32, (rows, LANES), 1)
    z_ref[...] = jnp.zeros(z_ref.shape, F32)

    @pl.when(pl.program_id(0) == 0)
    def _():
        p1_ref[...] = jnp.zeros(p1_ref.shape, F32)

    def reduce_rows(p_ref, row, tcol):
        rs = jnp.sum(p_ref[row:row + rows, :], axis=1, keepdims=True)
        z_ref[row:row + rows, :] = jnp.where(lane == tcol, rs, z_ref[row:row + rows, :])

    def one_token(t, p_ref, q_ref):
        hv = h_ref[t]
        h_hi = jnp.concatenate([hv[0:nj], hv[0:nj]], axis=0)
        h_lo = jnp.concatenate([hv[nj:2 * nj], hv[nj:2 * nj]], axis=0)
        ev = e_ref.at[pl.ds(t * nslot, nslot)]
        for i in range(0, nslot, 2):
            hi, lo = _unpack(_gather_pair(tbl_ref, ev, i, nj))
            p_ref[i * nj:i * nj + rows, :] = hi * h_hi + lo * h_lo
            reduce_rows(q_ref, i * nj, t - 1)

    def tok_body(i, carry):
        for d in range(0, U_TOKENS_PER_TRIP, 2):
            one_token(U_TOKENS_PER_TRIP * i + d, p0_ref, p1_ref)
            one_token(U_TOKENS_PER_TRIP * i + d + 1, p1_ref, p0_ref)
        return carry

    lax.fori_loop(0, tm // U_TOKENS_PER_TRIP, tok_body, 0)
    for r in range(0, nslot * nj, rows):
        reduce_rows(p1_ref, r, tm - 1)


def _peer_u_call(experts, h3, tbl, tm):
    ntok = h3.shape[0]
    nslot = experts.shape[0] // ntok
    nj = h3.shape[1] // 2
    kern = functools.partial(_peer_u_kernel, nslot=nslot, nj=nj)
    return pl.pallas_call(
        kern,
        name="peer_u",
        out_shape=jax.ShapeDtypeStruct((nslot * nj, ntok), F32),
        grid=(ntok // tm,),
        in_specs=[pl.BlockSpec((tm * nslot,), lambda i: (i,), memory_space=pltpu.SMEM),
                  pl.BlockSpec((tm, 2 * nj, LANES), lambda i: (i, 0, 0)),
                  _const_spec(tbl.shape)],
        out_specs=pl.BlockSpec((nslot * nj, tm), lambda i: (0, i)),
        scratch_shapes=[pltpu.VMEM((nslot * nj, LANES), F32),
                        pltpu.VMEM((nslot * nj, LANES), F32)],
        compiler_params=_cparams(("arbitrary",)),
    )(experts, h3, tbl)


def _act_kernel(z_ref, sel_ref, gate_ref, a_ref):
    zp = z_ref[...]
    zh = zp.astype(BF16)
    zl = (zp - zh.astype(F32)).astype(BF16)
    sel = sel_ref[...]
    z = _dot(sel, zh) + _dot(sel, zl)
    a = 0.5 * z * (1.0 + lax.erf(z * (1.0 / math.sqrt(2.0)))) * gate_ref[...]
    a_ref[...] = a.T


def _act_call(zpart, gates, tm):
    nslot, ntok = gates.shape
    nj = zpart.shape[0] // nslot
    sel = (jnp.arange(nslot * nj)[None, :] // nj == jnp.arange(nslot)[:, None]).astype(BF16)
    return pl.pallas_call(
        _act_kernel,
        name="peer_act",
        out_shape=jax.ShapeDtypeStruct((ntok, nslot), F32),
        grid=(ntok // tm,),
        in_specs=[pl.BlockSpec((nslot * nj, tm), lambda i: (0, i)),
                  _const_spec(sel.shape),
                  pl.BlockSpec((nslot, tm), lambda i: (0, i))],
        out_specs=pl.BlockSpec((tm, nslot), lambda i: (i, 0)),
        compiler_params=_cparams(("arbitrary",)),
    )(zpart, sel, gates)


def _peer_v_kernel(e_ref, act_ref, tbl_ref, y_ref, w0_ref, w1_ref, *, nslot, nj, nacc):
    tm = y_ref.shape[0]
    rows = 2 * nj
    upper = lax.broadcasted_iota(jnp.int32, (rows, LANES), 0) >= nj

    def expand(t, w_ref):
        w_ref[...] = jnp.broadcast_to(act_ref[pl.ds(t, 1), :], (nslot, nslot)).T

    def one_token(t, w_ref, wn_ref):
        expand(jnp.minimum(t + 1, tm - 1), wn_ref)
        ev = e_ref.at[pl.ds(t * nslot, nslot)]
        accs = [jnp.zeros((rows, LANES), F32) for _ in range(2 * nacc)]
        for i in range(0, nslot, 2):
            hi, lo = _unpack(_gather_pair(tbl_ref, ev, i, nj))
            a = jnp.where(upper, w_ref[i + 1:i + 2, :], w_ref[i:i + 1, :])
            j = (i // 2) % nacc
            accs[2 * j] = accs[2 * j] + a * hi
            accs[2 * j + 1] = accs[2 * j + 1] + a * lo
        acc_hi = sum(accs[2::2], accs[0])
        acc_lo = sum(accs[3::2], accs[1])
        y_ref[t] = jnp.concatenate([acc_hi[0:nj] + acc_hi[nj:], acc_lo[0:nj] + acc_lo[nj:]],
                                   axis=0)

    expand(0, w0_ref)

    def tok_body(i, carry):
        for d in range(0, V_TOKENS_PER_TRIP, 2):
            one_token(V_TOKENS_PER_TRIP * i + d, w0_ref, w1_ref)
            one_token(V_TOKENS_PER_TRIP * i + d + 1, w1_ref, w0_ref)
        return carry

    lax.fori_loop(0, tm // V_TOKENS_PER_TRIP, tok_body, 0)


def _peer_v_call(experts, act, tbl, nj, tm):
    ntok, nslot = act.shape
    assert nslot == LANES
    kern = functools.partial(_peer_v_kernel, nslot=nslot, nj=nj, nacc=2)
    return pl.pallas_call(
        kern,
        name="peer_v",
        out_shape=jax.ShapeDtypeStruct((ntok, 2 * nj, LANES), F32),
        grid=(ntok // tm,),
        in_specs=[pl.BlockSpec((tm * nslot,), lambda i: (i,), memory_space=pltpu.SMEM),
                  pl.BlockSpec((tm, nslot), lambda i: (i, 0)),
                  _const_spec(tbl.shape)],
        out_specs=pl.BlockSpec((tm, 2 * nj, LANES), lambda i: (i, 0, 0)),
        scratch_shapes=[pltpu.VMEM((nslot, LANES), F32), pltpu.VMEM((nslot, LANES), F32)],
        compiler_params=_cparams(("arbitrary",)),
    )(experts, act, tbl)


def _final_kernel(x_ref, y_ref, mod_ref, g_ref, b_ref, o_ref, *, dm, alpha):
    gate2 = mod_ref[0][:, 5 * dm:6 * dm]
    o_ref[0] = _ln(alpha * x_ref[0] + gate2 * y_ref[0], LN_EPS) * g_ref[...] + b_ref[...]


def _final_call(x1, y2, mod3, g, b, alpha, tm):
    bsz, seq, dm = x1.shape
    tok = pl.BlockSpec((1, tm, dm), lambda bi, s: (bi, s, 0))
    return pl.pallas_call(
        functools.partial(_final_kernel, dm=dm, alpha=alpha),
        name="final_ln",
        out_shape=jax.ShapeDtypeStruct((bsz, seq, dm), F32),
        grid=(bsz, seq // tm),
        in_specs=[tok, tok, pl.BlockSpec((1, 1, mod3.shape[-1]), lambda bi, s: (bi, 0, 0)),
                  _const_spec((1, dm)), _const_spec((1, dm))],
        out_specs=tok,
        compiler_params=_cparams(("arbitrary", "arbitrary")),
    )(x1, y2, mod3, g.reshape(1, -1), b.reshape(1, -1))


def kernel(x, c, cond_w, cond_b, w_in, mu_shift, conv_w, conv_b, conv_ln_g, conv_ln_b, rw_w0, rw_w2, rw_a0, rw_a2, rw_g2, rw_kk, rw_ka, rw_rk, rw_lnx_g, rw_lnx_b, w_out, ln1_g, ln1_b, peer_wq, peer_k1, peer_k2, peer_u, peer_v, ln2_g, ln2_b):
    bsz, seq, dm = x.shape
    nh, hd = rw_rk.shape
    rw = nh * hd
    lora_w, lora_a = rw_w2.shape[0], rw_a2.shape[0]
    assert lora_w == lora_a == LANES // 2 and rw_g2.shape[0] == LANES
    topk = PEER_TOPK
    alpha = (2.0 * DEPTH) ** 0.25
    tm = min(ROW_TILE, seq)
    chunk = min(SCAN_CHUNK, seq)

    mod3 = _mod_call(c, cond_w, cond_b).reshape(bsz, 1, -1)

    head_id = jnp.arange(rw) // hd
    bd = (head_id[:, None] == head_id[None, :]).astype(BF16)
    wa2 = jnp.zeros((LANES, 2 * rw), F32)
    wa2 = wa2.at[0:lora_w, 0:rw].set(rw_w2).at[lora_w:, rw:].set(rw_a2)

    rwpack, yconv, gb = _front_call(
        x, mod3, w_in.astype(BF16), mu_shift, conv_w, conv_b, conv_ln_g, conv_ln_b, rw_w0,
        rw_a0, wa2, rw_g2, rw_kk, rw_ka, rw_rk.reshape(-1), bd, min(FRONT_TILE, seq))
    yscan = _scan_call(rwpack, nh, hd, chunk)
    x1, h2, experts, gates = _post_call(
        x, yscan, yconv, gb, mod3, w_out.astype(BF16), rw_lnx_g, rw_lnx_b, ln1_g, ln1_b,
        peer_wq.astype(BF16), peer_k1, peer_k2, bd, hd, alpha, topk, tm)

    ntok = bsz * seq
    ptm = min(PEER_TOK, ntok)
    h3 = h2.reshape(ntok, dm // LANES, LANES)
    nj = dm // (2 * LANES)
    experts_t = (experts * nj).T.reshape(-1)
    zpart = _peer_u_call(experts_t, h3, _pack_call(peer_u), ptm)
    act = _act_call(zpart, gates, ptm)
    y3 = _peer_v_call(experts_t, act, _pack_call(peer_v), nj, ptm)
    y2 = y3.reshape(bsz, seq, dm)
    return _final_call(x1, y2, mod3, ln2_g, ln2_b, alpha, tm)
```

```python
import functools
import math

import jax
import jax.numpy as jnp
from jax import lax
from jax.experimental import pallas as pl
from jax.experimental.pallas import tpu as pltpu

F32 = jnp.float32
BF16 = jnp.bfloat16
HIGHEST = lax.Precision.HIGHEST

LN_EPS = 1e-5
GN_EPS = 64e-5
PEER_TOPK = 16
DEPTH = 1
LANES = 128
SUBLANES = 8
ROW_TILE = 256
FRONT_TILE = 256
SCAN_CHUNK = 64
PEER_TOK = 128
U_TOKENS_PER_TRIP = 8
V_TOKENS_PER_TRIP = 2
PACK_ROWS = 512
VMEM_LIMIT = 56 * 1024 * 1024


def _cparams(sem):
    return pltpu.CompilerParams(dimension_semantics=sem, vmem_limit_bytes=VMEM_LIMIT)


def _dot(a, b, precision=None):
    return jnp.dot(a, b, preferred_element_type=F32, precision=precision)


def _sigmoid(x):
    return 1.0 / (1.0 + jnp.exp(-x))


def _ln(x, eps):
    mu = jnp.mean(x, axis=-1, keepdims=True)
    xc = x - mu
    var = jnp.mean(xc * xc, axis=-1, keepdims=True)
    return xc * lax.rsqrt(var + eps)


def _head_sums(x, bd):
    hi = x.astype(BF16)
    lo = (x - hi.astype(F32)).astype(BF16)
    return _dot(hi, bd) + _dot(lo, bd)


def _const_spec(shape):
    nd = len(shape)
    return pl.BlockSpec(shape, lambda *_: (0,) * nd, pipeline_mode=pl.Buffered(1))


def _mod_kernel(c_ref, w_ref, b_ref, o_ref):
    c = c_ref[...]
    o_ref[...] = _dot(c * _sigmoid(c), w_ref[...], HIGHEST) + b_ref[...]


def _mod_call(c, cond_w, cond_b):
    bsz, dm = c.shape
    nblk = cond_w.shape[1] // dm
    return pl.pallas_call(
        _mod_kernel,
        name="mod",
        out_shape=jax.ShapeDtypeStruct((bsz, nblk * dm), F32),
        grid=(nblk,),
        in_specs=[pl.BlockSpec((bsz, dm), lambda j: (0, 0)),
                  pl.BlockSpec((dm, dm), lambda j: (0, j)),
                  pl.BlockSpec((1, dm), lambda j: (0, j))],
        out_specs=pl.BlockSpec((bsz, dm), lambda j: (0, j)),
        compiler_params=_cparams(("arbitrary",)),
    )(c, cond_w, cond_b.reshape(1, -1))


def _front_kernel(x_ref, mod_ref, win_ref, mu_ref, convw_ref, convb_ref, cg_ref, cb_ref,
                  w0_ref, a0_ref, wa2_ref, g2_ref, kkw_ref, kaw_ref, rk_ref, bd_ref,
                  rw_ref, yconv_ref, gb_ref, ubuf_ref, prev_ref, *, dm, cw, rw, ksize, halo):
    tm = x_ref.shape[1]
    first = pl.program_id(1) == 0

    @pl.when(first)
    def _():
        ubuf_ref[0:halo, :] = jnp.zeros((halo, cw), F32)
        prev_ref[...] = jnp.zeros(prev_ref.shape, F32)

    x = x_ref[0]
    mod = mod_ref[0]
    shift1 = mod[:, 0:dm]
    scale1 = mod[:, dm:2 * dm]
    h = _ln(x, LN_EPS) * (1.0 + scale1) + shift1
    p = _dot(h.astype(BF16), win_ref[...])

    u = p[:, 0:cw] * _sigmoid(p[:, cw:2 * cw])
    ubuf_ref[halo:halo + tm, :] = u
    acc = jnp.zeros((tm, cw), F32) + convb_ref[...]
    off = halo - (ksize - 1)
    ub = ubuf_ref[...]
    nrow = tm + halo
    for b in range(SUBLANES):
        xb = ub if b == 0 else pltpu.roll(ub, nrow - b, 0)
        for j in range(ksize):
            if (off + j) % SUBLANES == b:
                a8 = off + j - b
                acc = acc + convw_ref[j:j + 1, :] * xb[a8:a8 + tm, :]
    ubuf_ref[0:halo, :] = ubuf_ref[tm:tm + halo, :]
    yc = _ln(acc, LN_EPS) * cg_ref[...] + cb_ref[...]
    yconv_ref[0] = yc * _sigmoid(yc)

    prw = p[:, 2 * cw:]
    rolled = pltpu.roll(prw, 1, 0)
    row = lax.broadcasted_iota(jnp.int32, prw.shape, 0)
    p_prev = jnp.where(row == 0, prev_ref[0:1, :], rolled)
    prev_ref[0:1, :] = prw[tm - 1:tm, :]
    xm = prw + (p_prev - prw) * mu_ref[...]
    r = xm[:, 0:rw]
    k = xm[:, rw:2 * rw]
    v = xm[:, 2 * rw:3 * rw]
    wa = xm[:, 3 * rw:3 * rw + LANES]
    gd = xm[:, 3 * rw + LANES:3 * rw + 2 * LANES]
    lane = lax.broadcasted_iota(jnp.int32, wa.shape, 1)
    wa = jnp.where(lane < LANES // 2, jnp.tanh(wa), wa)
    t12 = _dot(wa, wa2_ref[...], HIGHEST)
    g = _dot(_sigmoid(gd), g2_ref[...], HIGHEST)
    y = w0_ref[...] + t12[:, 0:rw]
    sp = jnp.maximum(-y, 0.0) + jnp.log(1.0 + jnp.exp(-jnp.abs(y)))
    lw = -jnp.exp(-sp - 0.5)
    a = _sigmoid(a0_ref[...] + t12[:, rw:2 * rw])
    kk = k * kkw_ref[...]
    ss = _head_sums(kk * kk, bd_ref[...])
    kk = kk / jnp.maximum(jnp.sqrt(ss), 1e-12)
    kmod = k * (1.0 + (a - 1.0) * kaw_ref[...])
    bonus = _head_sums(r * kmod * rk_ref[...], bd_ref[...]) * v
    rw_ref[0, :, 0:rw] = r
    rw_ref[0, :, rw:2 * rw] = kmod
    rw_ref[0, :, 2 * rw:3 * rw] = v
    rw_ref[0, :, 3 * rw:4 * rw] = kk
    rw_ref[0, :, 4 * rw:5 * rw] = kk * a
    rw_ref[0, :, 5 * rw:6 * rw] = lw
    gb_ref[0, :, 0:rw] = g
    gb_ref[0, :, rw:2 * rw] = bonus


def _front_call(x, mod3, w_in, mu_shift, conv_w, conv_b, cg, cb, w0, a0, wa2, g2, kkw, kaw,
                rk, bd, tm):
    bsz, seq, dm = x.shape
    ksize, cw = conv_w.shape
    rw = w0.shape[-1]
    halo = 32
    assert ksize - 1 <= halo <= tm
    row = lambda a: a.reshape(1, -1)
    consts = [w_in, row(mu_shift), conv_w, row(conv_b), row(cg), row(cb), row(w0), row(a0),
              wa2, g2, row(kkw), row(kaw), row(rk), bd]
    kern = functools.partial(_front_kernel, dm=dm, cw=cw, rw=rw, ksize=ksize, halo=halo)
    return pl.pallas_call(
        kern,
        name="front",
        out_shape=(jax.ShapeDtypeStruct((bsz, seq, 6 * rw), F32),
                   jax.ShapeDtypeStruct((bsz, seq, cw), F32),
                   jax.ShapeDtypeStruct((bsz, seq, 2 * rw), F32)),
        grid=(bsz, seq // tm),
        in_specs=[pl.BlockSpec((1, tm, dm), lambda b, s: (b, s, 0)),
                  pl.BlockSpec((1, 1, mod3.shape[-1]), lambda b, s: (b, 0, 0))]
                 + [_const_spec(a.shape) for a in consts],
        out_specs=(pl.BlockSpec((1, tm, 6 * rw), lambda b, s: (b, s, 0)),
                   pl.BlockSpec((1, tm, cw), lambda b, s: (b, s, 0)),
                   pl.BlockSpec((1, tm, 2 * rw), lambda b, s: (b, s, 0))),
        scratch_shapes=[pltpu.VMEM((tm + halo, cw), F32),
                        pltpu.VMEM((8, w_in.shape[1] - 2 * cw), F32)],
        compiler_params=_cparams(("arbitrary", "arbitrary")),
    )(x, mod3, *consts)


SCAN_PASSES = 1
_NN = (((2,), (1,)), ((0,), (0,)))
_NT = (((2,), (2,)), ((0,), (0,)))
_TN = (((1,), (1,)), ((0,), (0,)))


def _split(x, passes=3):
    hi = x.astype(BF16)
    if passes == 1:
        return hi, None
    return hi, (x - hi.astype(F32)).astype(BF16)


def _mm3(a, b, dims):
    (ah, al), (bh, bl) = a, b
    dg = lambda p, q: lax.dot_general(p, q, dims, preferred_element_type=F32)
    out = dg(ah, bh)
    if bl is not None:
        out = out + dg(ah, bl)
    if al is not None:
        out = out + dg(al, bh)
    return out


def _scan_kernel(rw_ref, tri_ref, y_ref, state_ref, *, nh, hd):
    L = rw_ref.shape[1]
    rwid = nh * hd

    @pl.when(pl.program_id(1) == 0)
    def _():
        state_ref[...] = jnp.zeros(state_ref.shape, F32)

    blk = rw_ref[0]
    tri = tri_ref[...]
    c = _dot(tri, blk[:, 5 * rwid:6 * rwid], HIGHEST)
    lw = blk[:, 5 * rwid:6 * rwid]
    e_pos = jnp.exp(c)
    e_neg = jnp.exp(-c)
    rt_all = blk[:, 0:rwid] * e_pos
    kh_all = blk[:, rwid:2 * rwid] * e_neg
    kt_all = blk[:, 3 * rwid:4 * rwid] * jnp.exp(c - lw)
    bh_all = blk[:, 4 * rwid:5 * rwid] * e_neg
    v_all = blk[:, 2 * rwid:3 * rwid]
    pl_all = e_pos[L - 1:L, :]

    def heads(t):
        return jnp.stack([t[:, h * hd:(h + 1) * hd] for h in range(nh)], axis=0)

    rt, kh, kt, bh, v = heads(rt_all), heads(kh_all), heads(kt_all), heads(bh_all), heads(v_all)
    plh = heads(pl_all)
    s0 = state_ref[...]

    strict = (tri - jnp.eye(L, dtype=F32))[None]
    incl = tri[None]
    sp = functools.partial(_split, passes=SCAN_PASSES)
    kt_s, bh_s, kh_s, rt_s, v_s, s0_s = (sp(t) for t in (kt, bh, kh, rt, v, s0))
    t_bb = _mm3(kt_s, bh_s, _NT) * strict
    t_bk = _mm3(kt_s, kh_s, _NT) * strict
    g_rb = _mm3(rt_s, bh_s, _NT) * incl
    g_rk = _mm3(rt_s, kh_s, _NT) * incl

    w = _mm3(kt_s, s0_s, _NT) + _mm3(sp(t_bk), v_s, _NN)
    tp_s = sp(t_bb)
    w = w - _mm3(tp_s, sp(w), _NN)
    n = 2
    while n < L:
        tp_s = sp(_mm3(tp_s, tp_s, _NN))
        w = w + _mm3(tp_s, sp(w), _NN)
        n *= 2
    u_s = sp(w)
    y = (_mm3(rt_s, s0_s, _NT) - _mm3(sp(g_rb), u_s, _NN)
         + _mm3(sp(g_rk), v_s, _NN))
    state_ref[...] = (s0 * plh + _mm3(v_s, sp(kh * plh), _TN)
                      - _mm3(u_s, sp(bh * plh), _TN))
    y_ref[0] = jnp.concatenate([y[h] for h in range(nh)], axis=-1)


def _scan_call(rwpack, nh, hd, chunk):
    bsz, seq, w6 = rwpack.shape
    rwid = nh * hd
    tri = jnp.tril(jnp.ones((chunk, chunk), F32))
    kern = functools.partial(_scan_kernel, nh=nh, hd=hd)
    return pl.pallas_call(
        kern,
        name="rwkv_scan",
        out_shape=jax.ShapeDtypeStruct((bsz, seq, rwid), F32),
        grid=(bsz, seq // chunk),
        in_specs=[pl.BlockSpec((1, chunk, w6), lambda b, s: (b, s, 0)),
                  _const_spec(tri.shape)],
        out_specs=pl.BlockSpec((1, chunk, rwid), lambda b, s: (b, s, 0)),
        scratch_shapes=[pltpu.VMEM((nh, hd, hd), F32)],
        compiler_params=_cparams(("arbitrary", "arbitrary")),
    )(rwpack, tri)


def _extract_max(s, iota, fill):
    m = jnp.max(s, axis=0, keepdims=True)
    idx = jnp.min(jnp.where(s == m, iota, fill), axis=0, keepdims=True)
    return m, idx, jnp.where(iota == idx, -jnp.inf, s)


def _topk_cols(s, k):
    n = s.shape[0]
    iota = lax.broadcasted_iota(jnp.int32, s.shape, 0)
    vals, idxs = [], []
    for _ in range(k):
        m, idx, s = _extract_max(s, iota, n)
        vals.append(m)
        idxs.append(idx)
    return jnp.concatenate(vals, axis=0), jnp.concatenate(idxs, axis=0)


def _post_kernel(x_ref, y_ref, yconv_ref, gb_ref, mod_ref, wout_ref, lnxg_ref, lnxb_ref,
                 ln1g_ref, ln1b_ref, wq_ref, k1_ref, k2_ref, bd_ref,
                 x1_ref, h2_ref, e_ref, gate_ref, q_ref, *, dm, rw, hd, alpha, nkeys, topk):
    tm = x_ref.shape[1]
    nheads = k1_ref.shape[0]
    half = k1_ref.shape[2]
    x = x_ref[0]
    mod = mod_ref[0]
    gate1 = mod[:, 2 * dm:3 * dm]
    shift2 = mod[:, 3 * dm:4 * dm]
    scale2 = mod[:, 4 * dm:5 * dm]

    y = y_ref[0]
    bd = bd_ref[...]
    mu = _head_sums(y, bd) * (1.0 / hd)
    yc = y - mu
    var = _head_sums(yc * yc, bd) * (1.0 / hd)
    gn = yc * lax.rsqrt(var + GN_EPS)
    gb = gb_ref[0]
    y_rw = (gn * lnxg_ref[...] + lnxb_ref[...] + gb[:, rw:2 * rw]) * gb[:, 0:rw]
    ycat = jnp.concatenate([yconv_ref[0], y_rw], axis=-1)
    y1 = _dot(ycat.astype(BF16), wout_ref[...])
    x1 = _ln(alpha * x + gate1 * y1, LN_EPS) * ln1g_ref[...] + ln1b_ref[...]
    x1_ref[0] = x1
    h2 = _ln(x1, LN_EPS) * (1.0 + scale2) + shift2
    h2_ref[0] = h2
    q = _dot(h2.astype(BF16), wq_ref[...])
    for i in range(2 * nheads):
        q_ref[i] = q[:, i * half:(i + 1) * half]

    sub = 8
    assert topk == 2 * sub
    r16 = lax.broadcasted_iota(jnp.int32, (topk, LANES), 0)
    r8 = lax.broadcasted_iota(jnp.int32, (sub, LANES), 0)
    flat = jnp.concatenate([r16] + [a * topk + r8 for a in range(1, sub)] + [(sub + r8) * topk],
                           axis=0)
    ncand = topk * topk

    def route(h, c0):
        s1 = lax.dot_general(k1_ref[h], q_ref[2 * h, c0:c0 + LANES, :], (((1,), (1,)), ((), ())),
                             preferred_element_type=F32, precision=HIGHEST)
        s2 = lax.dot_general(k2_ref[h], q_ref[2 * h + 1, c0:c0 + LANES, :],
                             (((1,), (1,)), ((), ())),
                             preferred_element_type=F32, precision=HIGHEST)
        v1, i1 = _topk_cols(s1, topk)
        v2, i2 = _topk_cols(s2, topk)
        cand = jnp.concatenate([v1[0:1, :] + v2]
                               + [v1[a:a + 1, :] + v2[0:sub, :] for a in range(1, sub)]
                               + [v1[sub:, :] + v2[0:1, :]], axis=0)
        eall = jnp.concatenate([i1[0:1, :] * nkeys + i2]
                               + [i1[a:a + 1, :] * nkeys + i2[0:sub, :] for a in range(1, sub)]
                               + [i1[sub:, :] * nkeys + i2[0:1, :]], axis=0)
        scs, exs = [], []
        for _ in range(topk):
            m, idx, cand_next = _extract_max(cand, flat, ncand)
            exs.append(jnp.max(jnp.where(flat == idx, eall, -1), axis=0, keepdims=True))
            scs.append(m)
            cand = cand_next
        sc = jnp.concatenate(scs, axis=0)
        ex = jnp.concatenate(exs, axis=0)
        pexp = jnp.exp(sc - sc[0:1, :])
        gates = pexp / jnp.sum(pexp, axis=0, keepdims=True)
        base = pl.multiple_of(h * topk, topk)
        e_ref[pl.ds(base, topk), c0:c0 + LANES] = ex
        gate_ref[pl.ds(base, topk), c0:c0 + LANES] = gates

    def head_body(h, carry):
        for c0 in range(0, tm, LANES):
            route(h, c0)
        return carry

    lax.fori_loop(0, nheads, head_body, 0)


def _post_call(x, yscan, yconv, gb, mod3, w_out, lnxg, lnxb, ln1g, ln1b, wq, k1, k2, bd,
               hd, alpha, topk, tm):
    bsz, seq, dm = x.shape
    rw = yscan.shape[-1]
    cw = yconv.shape[-1]
    nheads, nkeys, half = k1.shape
    nslot = nheads * topk
    ntok = bsz * seq
    nst = seq // tm
    row = lambda a: a.reshape(1, -1)
    consts = [w_out, row(lnxg), row(lnxb), row(ln1g), row(ln1b), wq, k1, k2, bd]
    kern = functools.partial(_post_kernel, dm=dm, rw=rw, hd=hd, alpha=alpha, nkeys=nkeys,
                             topk=topk)
    tok = lambda w: pl.BlockSpec((1, tm, w), lambda b, s: (b, s, 0))
    slot = pl.BlockSpec((nslot, tm), lambda b, s: (0, b * nst + s))
    return pl.pallas_call(
        kern,
        name="post_route",
        out_shape=(jax.ShapeDtypeStruct((bsz, seq, dm), F32),
                   jax.ShapeDtypeStruct((bsz, seq, dm), F32),
                   jax.ShapeDtypeStruct((nslot, ntok), jnp.int32),
                   jax.ShapeDtypeStruct((nslot, ntok), F32)),
        grid=(bsz, nst),
        in_specs=[tok(dm), tok(rw), tok(cw), tok(2 * rw),
                  pl.BlockSpec((1, 1, mod3.shape[-1]), lambda b, s: (b, 0, 0))]
                 + [_const_spec(a.shape) for a in consts],
        out_specs=(tok(dm), tok(dm), slot, slot),
        scratch_shapes=[pltpu.VMEM((2 * nheads, tm, half), F32)],
        compiler_params=_cparams(("arbitrary", "arbitrary")),
    )(x, yscan, yconv, gb, mod3, *consts)


def _pack_kernel(t_ref, o_ref):
    t = t_ref[...]
    hw = t.shape[1] // 2
    hi = lax.bitcast_convert_type(t[:, 0:hw].astype(BF16).astype(F32), jnp.uint32)
    lo = lax.bitcast_convert_type(t[:, hw:].astype(BF16).astype(F32), jnp.uint32)
    o_ref[...] = (hi & jnp.uint32(0xFFFF0000)) | (lo >> 16)


def _pack_call(table):
    ne, dm = table.shape
    packed = pl.pallas_call(
        _pack_kernel,
        name="peer_pack",
        out_shape=jax.ShapeDtypeStruct((ne, dm // 2), jnp.uint32),
        grid=(ne // PACK_ROWS,),
        in_specs=[pl.BlockSpec((PACK_ROWS, dm), lambda i: (i, 0))],
        out_specs=pl.BlockSpec((PACK_ROWS, dm // 2), lambda i: (i, 0)),
        compiler_params=_cparams(("arbitrary",)),
    )(table)
    return packed.reshape(ne * (dm // (2 * LANES)), LANES)


def _unpack(w):
    hi = lax.bitcast_convert_type(w & jnp.uint32(0xFFFF0000), F32)
    lo = lax.bitcast_convert_type(w << 16, F32)
    return hi, lo


def _gather_pair(tbl_ref, e_ref, off, nj):
    rows = [tbl_ref[pl.ds(pl.multiple_of(e_ref[off + d], nj), nj), :] for d in range(2)]
    return jnp.concatenate(rows, axis=0)


def _stack_chunks(row, first, nj):
    sub = lax.broadcasted_iota(jnp.int32, (2 * nj, LANES), 0) % nj
    chunk = lambda c: jnp.broadcast_to(row[:, c * LANES:(c + 1) * LANES], (2 * nj, LANES))
    out = chunk(first)
    for q in range(1, nj):
        out = jnp.where(sub == q, chunk(first + q), out)
    return out


def _peer_u_kernel(e_ref, h_ref, tbl_ref, z_ref, p0_ref, p1_ref, *, nslot, nj):
    tm = h_ref.shape[0]
    rows = 2 * nj
    lane = lax.broadcasted_iota(jnp.int32, (rows, LANES), 1)
    z_ref[...] = jnp.zeros(z_ref.shape, F32)

    @pl.when(pl.program_id(0) == 0)
    def _():
        p1_ref[...] = jnp.zeros(p1_ref.shape, F32)

    def reduce_rows(p_ref, row, tcol):
        rs = jnp.sum(p_ref[row:row + rows, :], axis=1, keepdims=True)
        z_ref[row:row + rows, :] = jnp.where(lane == tcol, rs, z_ref[row:row + rows, :])

    def one_token(t, p_ref, q_ref):
        hrow = h_ref[pl.ds(t, 1), :]
        h_hi = _stack_chunks(hrow, 0, nj)
        h_lo = _stack_chunks(hrow, nj, nj)
        ev = e_ref.at[pl.ds(t * nslot, nslot)]
        for i in range(0, nslot, 2):
            hi, lo = _unpack(_gather_pair(tbl_ref, ev, i, nj))
            p_ref[i * nj:i * nj + rows, :] = hi * h_hi + lo * h_lo
            reduce_rows(q_ref, i * nj, t - 1)

    def tok_body(i, carry):
        for d in range(0, U_TOKENS_PER_TRIP, 2):
            one_token(U_TOKENS_PER_TRIP * i + d, p0_ref, p1_ref)
            one_token(U_TOKENS_PER_TRIP * i + d + 1, p1_ref, p0_ref)
        return carry

    lax.fori_loop(0, tm // U_TOKENS_PER_TRIP, tok_body, 0)
    for r in range(0, nslot * nj, rows):
        reduce_rows(p1_ref, r, tm - 1)


def _peer_u_call(experts, h2, tbl, tm):
    ntok, dm = h2.shape
    nslot = experts.shape[0] // ntok
    nj = dm // (2 * LANES)
    kern = functools.partial(_peer_u_kernel, nslot=nslot, nj=nj)
    return pl.pallas_call(
        kern,
        name="peer_u",
        out_shape=jax.ShapeDtypeStruct((nslot * nj, ntok), F32),
        grid=(ntok // tm,),
        in_specs=[pl.BlockSpec((tm * nslot,), lambda i: (i,), memory_space=pltpu.SMEM),
                  pl.BlockSpec((tm, dm), lambda i: (i, 0)),
                  _const_spec(tbl.shape)],
        out_specs=pl.BlockSpec((nslot * nj, tm), lambda i: (0, i)),
        scratch_shapes=[pltpu.VMEM((nslot * nj, LANES), F32),
                        pltpu.VMEM((nslot * nj, LANES), F32)],
        compiler_params=_cparams(("arbitrary",)),
    )(experts, h2, tbl)


def _act_kernel(z_ref, sel_ref, gate_ref, a_ref):
    zp = z_ref[...]
    zh = zp.astype(BF16)
    zl = (zp - zh.astype(F32)).astype(BF16)
    sel = sel_ref[...]
    z = _dot(sel, zh) + _dot(sel, zl)
    a = 0.5 * z * (1.0 + lax.erf(z * (1.0 / math.sqrt(2.0)))) * gate_ref[...]
    a_ref[...] = a.T


def _act_call(zpart, gates, tm):
    nslot, ntok = gates.shape
    nj = zpart.shape[0] // nslot
    sel = (jnp.arange(nslot * nj)[None, :] // nj == jnp.arange(nslot)[:, None]).astype(BF16)
    return pl.pallas_call(
        _act_kernel,
        name="peer_act",
        out_shape=jax.ShapeDtypeStruct((ntok, nslot), F32),
        grid=(ntok // tm,),
        in_specs=[pl.BlockSpec((nslot * nj, tm), lambda i: (0, i)),
                  _const_spec(sel.shape),
                  pl.BlockSpec((nslot, tm), lambda i: (0, i))],
        out_specs=pl.BlockSpec((tm, nslot), lambda i: (i, 0)),
        compiler_params=_cparams(("arbitrary",)),
    )(zpart, sel, gates)


def _peer_v_kernel(e_ref, act_ref, tbl_ref, y_ref, w0_ref, w1_ref, *, nslot, nj, nacc):
    tm = y_ref.shape[0]
    rows = 2 * nj
    upper = lax.broadcasted_iota(jnp.int32, (rows, LANES), 0) >= nj

    def expand(t, w_ref):
        w_ref[...] = jnp.broadcast_to(act_ref[pl.ds(t, 1), :], (nslot, nslot)).T

    def one_token(t, w_ref, wn_ref):
        expand(jnp.minimum(t + 1, tm - 1), wn_ref)
        ev = e_ref.at[pl.ds(t * nslot, nslot)]
        accs = [jnp.zeros((rows, LANES), F32) for _ in range(2 * nacc)]
        for i in range(0, nslot, 2):
            hi, lo = _unpack(_gather_pair(tbl_ref, ev, i, nj))
            a = jnp.where(upper, w_ref[i + 1:i + 2, :], w_ref[i:i + 1, :])
            j = (i // 2) % nacc
            accs[2 * j] = accs[2 * j] + a * hi
            accs[2 * j + 1] = accs[2 * j + 1] + a * lo
        acc_hi = sum(accs[2::2], accs[0])
        acc_lo = sum(accs[3::2], accs[1])
        y_ref[t] = jnp.concatenate([acc_hi[0:nj] + acc_hi[nj:], acc_lo[0:nj] + acc_lo[nj:]],
                                   axis=0)

    expand(0, w0_ref)

    def tok_body(i, carry):
        for d in range(0, V_TOKENS_PER_TRIP, 2):
            one_token(V_TOKENS_PER_TRIP * i + d, w0_ref, w1_ref)
            one_token(V_TOKENS_PER_TRIP * i + d + 1, w1_ref, w0_ref)
        return carry

    lax.fori_loop(0, tm // V_TOKENS_PER_TRIP, tok_body, 0)


def _peer_v_call(experts, act, tbl, nj, tm):
    ntok, nslot = act.shape
    assert nslot == LANES
    kern = functools.partial(_peer_v_kernel, nslot=nslot, nj=nj, nacc=2)
    return pl.pallas_call(
        kern,
        name="peer_v",
        out_shape=jax.ShapeDtypeStruct((ntok, 2 * nj, LANES), F32),
        grid=(ntok // tm,),
        in_specs=[pl.BlockSpec((tm * nslot,), lambda i: (i,), memory_space=pltpu.SMEM),
                  pl.BlockSpec((tm, nslot), lambda i: (i, 0)),
                  _const_spec(tbl.shape)],
        out_specs=pl.BlockSpec((tm, 2 * nj, LANES), lambda i: (i, 0, 0)),
        scratch_shapes=[pltpu.VMEM((nslot, LANES), F32), pltpu.VMEM((nslot, LANES), F32)],
        compiler_params=_cparams(("arbitrary",)),
    )(experts, act, tbl)


def _final_kernel(x_ref, y_ref, mod_ref, g_ref, b_ref, o_ref, *, dm, alpha):
    gate2 = mod_ref[0][:, 5 * dm:6 * dm]
    o_ref[0] = _ln(alpha * x_ref[0] + gate2 * y_ref[0], LN_EPS) * g_ref[...] + b_ref[...]


def _final_call(x1, y2, mod3, g, b, alpha, tm):
    bsz, seq, dm = x1.shape
    tok = pl.BlockSpec((1, tm, dm), lambda bi, s: (bi, s, 0))
    return pl.pallas_call(
        functools.partial(_final_kernel, dm=dm, alpha=alpha),
        name="final_ln",
        out_shape=jax.ShapeDtypeStruct((bsz, seq, dm), F32),
        grid=(bsz, seq // tm),
        in_specs=[tok, tok, pl.BlockSpec((1, 1, mod3.shape[-1]), lambda bi, s: (bi, 0, 0)),
                  _const_spec((1, dm)), _const_spec((1, dm))],
        out_specs=tok,
        compiler_params=_cparams(("arbitrary", "arbitrary")),
    )(x1, y2, mod3, g.reshape(1, -1), b.reshape(1, -1))


def kernel(x, c, cond_w, cond_b, w_in, mu_shift, conv_w, conv_b, conv_ln_g, conv_ln_b, rw_w0, rw_w2, rw_a0, rw_a2, rw_g2, rw_kk, rw_ka, rw_rk, rw_lnx_g, rw_lnx_b, w_out, ln1_g, ln1_b, peer_wq, peer_k1, peer_k2, peer_u, peer_v, ln2_g, ln2_b):
    bsz, seq, dm = x.shape
    nh, hd = rw_rk.shape
    rw = nh * hd
    lora_w, lora_a = rw_w2.shape[0], rw_a2.shape[0]
    assert lora_w == lora_a == LANES // 2 and rw_g2.shape[0] == LANES
    topk = PEER_TOPK
    alpha = (2.0 * DEPTH) ** 0.25
    tm = min(ROW_TILE, seq)
    chunk = min(SCAN_CHUNK, seq)

    mod3 = _mod_call(c, cond_w, cond_b).reshape(bsz, 1, -1)

    head_id = jnp.arange(rw) // hd
    bd = (head_id[:, None] == head_id[None, :]).astype(BF16)
    wa2 = jnp.zeros((LANES, 2 * rw), F32)
    wa2 = wa2.at[0:lora_w, 0:rw].set(rw_w2).at[lora_w:, rw:].set(rw_a2)

    rwpack, yconv, gb = _front_call(
        x, mod3, w_in.astype(BF16), mu_shift, conv_w, conv_b, conv_ln_g, conv_ln_b, rw_w0,
        rw_a0, wa2, rw_g2, rw_kk, rw_ka, rw_rk.reshape(-1), bd, min(FRONT_TILE, seq))
    yscan = _scan_call(rwpack, nh, hd, chunk)
    x1, h2, experts, gates = _post_call(
        x, yscan, yconv, gb, mod3, w_out.astype(BF16), rw_lnx_g, rw_lnx_b, ln1_g, ln1_b,
        peer_wq.astype(BF16), peer_k1, peer_k2, bd, hd, alpha, topk, tm)

    ntok = bsz * seq
    ptm = min(PEER_TOK, ntok)
    nj = dm // (2 * LANES)
    experts_t = (experts * nj).T.reshape(-1)
    zpart = _peer_u_call(experts_t, h2.reshape(ntok, dm), _pack_call(peer_u), ptm)
    act = _act_call(zpart, gates, ptm)
    y3 = _peer_v_call(experts_t, act, _pack_call(peer_v), nj, ptm)
    y2 = y3.reshape(bsz, seq, dm)
    return _final_call(x1, y2, mod3, ln2_g, ln2_b, alpha, tm)
```

```python
import functools
import math

import jax
import jax.numpy as jnp
from jax import lax
from jax.experimental import pallas as pl
from jax.experimental.pallas import tpu as pltpu

F32 = jnp.float32
BF16 = jnp.bfloat16
HIGHEST = lax.Precision.HIGHEST

LN_EPS = 1e-5
GN_EPS = 64e-5
PEER_TOPK = 16
DEPTH = 1
LANES = 128
SUBLANES = 8
ROW_TILE = 256
FRONT_TILE = 256
SCAN_CHUNK = 64
SCAN_ROWS = 4
PEER_TOK = 128
U_TOKENS_PER_TRIP = 8
V_TOKENS_PER_TRIP = 2
PACK_ROWS = 512
VMEM_LIMIT = 56 * 1024 * 1024


def _cparams(sem):
    return pltpu.CompilerParams(dimension_semantics=sem, vmem_limit_bytes=VMEM_LIMIT)


def _dot(a, b, precision=None):
    return jnp.dot(a, b, preferred_element_type=F32, precision=precision)


def _sigmoid(x):
    return 1.0 / (1.0 + jnp.exp(-x))


def _ln(x, eps):
    mu = jnp.mean(x, axis=-1, keepdims=True)
    xc = x - mu
    var = jnp.mean(xc * xc, axis=-1, keepdims=True)
    return xc * lax.rsqrt(var + eps)


def _head_sums(x, bd):
    hi = x.astype(BF16)
    lo = (x - hi.astype(F32)).astype(BF16)
    return _dot(hi, bd) + _dot(lo, bd)


def _const_spec(shape):
    nd = len(shape)
    return pl.BlockSpec(shape, lambda *_: (0,) * nd, pipeline_mode=pl.Buffered(1))


def _mod_kernel(c_ref, w_ref, b_ref, o_ref):
    c = c_ref[...]
    o_ref[...] = _dot(c * _sigmoid(c), w_ref[...], HIGHEST) + b_ref[...]


def _mod_call(c, cond_w, cond_b):
    bsz, dm = c.shape
    nblk = cond_w.shape[1] // dm
    return pl.pallas_call(
        _mod_kernel,
        name="mod",
        out_shape=jax.ShapeDtypeStruct((bsz, nblk * dm), F32),
        grid=(nblk,),
        in_specs=[pl.BlockSpec((bsz, dm), lambda j: (0, 0)),
                  pl.BlockSpec((dm, dm), lambda j: (0, j)),
                  pl.BlockSpec((1, dm), lambda j: (0, j))],
        out_specs=pl.BlockSpec((bsz, dm), lambda j: (0, j)),
        compiler_params=_cparams(("arbitrary",)),
    )(c, cond_w, cond_b.reshape(1, -1))


def _front_kernel(x_ref, mod_ref, win_ref, mu_ref, convw_ref, convb_ref, cg_ref, cb_ref,
                  w0_ref, a0_ref, wa2_ref, g2_ref, kkw_ref, kaw_ref, rk_ref, bd_ref,
                  rw_ref, yconv_ref, gb_ref, ubuf_ref, prev_ref, *, dm, cw, rw, ksize, halo):
    tm = x_ref.shape[1]
    first = pl.program_id(1) == 0

    @pl.when(first)
    def _():
        ubuf_ref[0:halo, :] = jnp.zeros((halo, cw), F32)
        prev_ref[...] = jnp.zeros(prev_ref.shape, F32)

    x = x_ref[0]
    mod = mod_ref[0]
    shift1 = mod[:, 0:dm]
    scale1 = mod[:, dm:2 * dm]
    h = _ln(x, LN_EPS) * (1.0 + scale1) + shift1
    p = _dot(h.astype(BF16), win_ref[...])

    u = p[:, 0:cw] * _sigmoid(p[:, cw:2 * cw])
    ubuf_ref[halo:halo + tm, :] = u
    acc = jnp.zeros((tm, cw), F32) + convb_ref[...]
    off = halo - (ksize - 1)
    ub = ubuf_ref[...]
    nrow = tm + halo
    for b in range(SUBLANES):
        xb = ub if b == 0 else pltpu.roll(ub, nrow - b, 0)
        for j in range(ksize):
            if (off + j) % SUBLANES == b:
                a8 = off + j - b
                acc = acc + convw_ref[j:j + 1, :] * xb[a8:a8 + tm, :]
    ubuf_ref[0:halo, :] = ubuf_ref[tm:tm + halo, :]
    yc = _ln(acc, LN_EPS) * cg_ref[...] + cb_ref[...]
    yconv_ref[0] = yc * _sigmoid(yc)

    prw = p[:, 2 * cw:]
    rolled = pltpu.roll(prw, 1, 0)
    row = lax.broadcasted_iota(jnp.int32, prw.shape, 0)
    p_prev = jnp.where(row == 0, prev_ref[0:1, :], rolled)
    prev_ref[0:1, :] = prw[tm - 1:tm, :]
    xm = prw + (p_prev - prw) * mu_ref[...]
    r = xm[:, 0:rw]
    k = xm[:, rw:2 * rw]
    v = xm[:, 2 * rw:3 * rw]
    wa = xm[:, 3 * rw:3 * rw + LANES]
    gd = xm[:, 3 * rw + LANES:3 * rw + 2 * LANES]
    lane = lax.broadcasted_iota(jnp.int32, wa.shape, 1)
    wa = jnp.where(lane < LANES // 2, jnp.tanh(wa), wa)
    t12 = _dot(wa, wa2_ref[...], HIGHEST)
    g = _dot(_sigmoid(gd), g2_ref[...], HIGHEST)
    y = w0_ref[...] + t12[:, 0:rw]
    sp = jnp.maximum(-y, 0.0) + jnp.log(1.0 + jnp.exp(-jnp.abs(y)))
    lw = -jnp.exp(-sp - 0.5)
    a = _sigmoid(a0_ref[...] + t12[:, rw:2 * rw])
    kk = k * kkw_ref[...]
    ss = _head_sums(kk * kk, bd_ref[...])
    kk = kk / jnp.maximum(jnp.sqrt(ss), 1e-12)
    kmod = k * (1.0 + (a - 1.0) * kaw_ref[...])
    bonus = _head_sums(r * kmod * rk_ref[...], bd_ref[...]) * v
    rw_ref[0, :, 0:rw] = r
    rw_ref[0, :, rw:2 * rw] = kmod
    rw_ref[0, :, 2 * rw:3 * rw] = v
    rw_ref[0, :, 3 * rw:4 * rw] = kk
    rw_ref[0, :, 4 * rw:5 * rw] = kk * a
    rw_ref[0, :, 5 * rw:6 * rw] = lw
    gb_ref[0, :, 0:rw] = g
    gb_ref[0, :, rw:2 * rw] = bonus


def _front_call(x, mod3, w_in, mu_shift, conv_w, conv_b, cg, cb, w0, a0, wa2, g2, kkw, kaw,
                rk, bd, tm):
    bsz, seq, dm = x.shape
    ksize, cw = conv_w.shape
    rw = w0.shape[-1]
    halo = 32
    assert ksize - 1 <= halo <= tm
    row = lambda a: a.reshape(1, -1)
    consts = [w_in, row(mu_shift), conv_w, row(conv_b), row(cg), row(cb), row(w0), row(a0),
              wa2, g2, row(kkw), row(kaw), row(rk), bd]
    kern = functools.partial(_front_kernel, dm=dm, cw=cw, rw=rw, ksize=ksize, halo=halo)
    return pl.pallas_call(
        kern,
        name="front",
        out_shape=(jax.ShapeDtypeStruct((bsz, seq, 6 * rw), F32),
                   jax.ShapeDtypeStruct((bsz, seq, cw), F32),
                   jax.ShapeDtypeStruct((bsz, seq, 2 * rw), F32)),
        grid=(bsz, seq // tm),
        in_specs=[pl.BlockSpec((1, tm, dm), lambda b, s: (b, s, 0)),
                  pl.BlockSpec((1, 1, mod3.shape[-1]), lambda b, s: (b, 0, 0))]
                 + [_const_spec(a.shape) for a in consts],
        out_specs=(pl.BlockSpec((1, tm, 6 * rw), lambda b, s: (b, s, 0)),
                   pl.BlockSpec((1, tm, cw), lambda b, s: (b, s, 0)),
                   pl.BlockSpec((1, tm, 2 * rw), lambda b, s: (b, s, 0))),
        scratch_shapes=[pltpu.VMEM((tm + halo, cw), F32),
                        pltpu.VMEM((8, w_in.shape[1] - 2 * cw), F32)],
        compiler_params=_cparams(("arbitrary", "arbitrary")),
    )(x, mod3, *consts)


SCAN_PASSES = 1
_NN = (((2,), (1,)), ((0,), (0,)))
_NT = (((2,), (2,)), ((0,), (0,)))
_TN = (((1,), (1,)), ((0,), (0,)))


def _split(x, passes=3):
    hi = x.astype(BF16)
    if passes == 1:
        return hi, None
    return hi, (x - hi.astype(F32)).astype(BF16)


def _mm3(a, b, dims):
    (ah, al), (bh, bl) = a, b
    dg = lambda p, q: lax.dot_general(p, q, dims, preferred_element_type=F32)
    out = dg(ah, bh)
    if bl is not None:
        out = out + dg(ah, bl)
    if al is not None:
        out = out + dg(al, bh)
    return out


def _scan_kernel(rw_ref, tri_ref, y_ref, state_ref, *, nh, hd):
    L = rw_ref.shape[1]
    rwid = nh * hd

    @pl.when(pl.program_id(1) == 0)
    def _():
        state_ref[...] = jnp.zeros(state_ref.shape, F32)

    nrow = rw_ref.shape[0]
    tri = tri_ref[...]
    parts = [[] for _ in range(6)]
    for b in range(nrow):
        blk = rw_ref[b]
        lw = blk[:, 5 * rwid:6 * rwid]
        c = _dot(tri, lw, HIGHEST)
        e_pos = jnp.exp(c)
        e_neg = jnp.exp(-c)
        parts[0].append(blk[:, 0:rwid] * e_pos)
        parts[1].append(blk[:, rwid:2 * rwid] * e_neg)
        parts[2].append(blk[:, 3 * rwid:4 * rwid] * jnp.exp(c - lw))
        parts[3].append(blk[:, 4 * rwid:5 * rwid] * e_neg)
        parts[4].append(blk[:, 2 * rwid:3 * rwid])
        parts[5].append(e_pos[L - 1:L, :])

    def heads(ts):
        return jnp.stack([t[:, h * hd:(h + 1) * hd] for t in ts for h in range(nh)], axis=0)

    rt, kh, kt, bh, v, plh = (heads(p) for p in parts)
    s0 = state_ref[...]

    strict = (tri - jnp.eye(L, dtype=F32))[None]
    incl = tri[None]
    sp = functools.partial(_split, passes=SCAN_PASSES)
    kt_s, bh_s, kh_s, rt_s, v_s, s0_s = (sp(t) for t in (kt, bh, kh, rt, v, s0))
    t_bb = _mm3(kt_s, bh_s, _NT) * strict
    t_bk = _mm3(kt_s, kh_s, _NT) * strict
    g_rb = _mm3(rt_s, bh_s, _NT) * incl
    g_rk = _mm3(rt_s, kh_s, _NT) * incl

    w = _mm3(kt_s, s0_s, _NT) + _mm3(sp(t_bk), v_s, _NN)
    tp_s = sp(t_bb)
    w = w - _mm3(tp_s, sp(w), _NN)
    n = 2
    while n < L:
        tp_s = sp(_mm3(tp_s, tp_s, _NN))
        w = w + _mm3(tp_s, sp(w), _NN)
        n *= 2
    u_s = sp(w)
    y = (_mm3(rt_s, s0_s, _NT) - _mm3(sp(g_rb), u_s, _NN)
         + _mm3(sp(g_rk), v_s, _NN))
    state_ref[...] = (s0 * plh + _mm3(v_s, sp(kh * plh), _TN)
                      - _mm3(u_s, sp(bh * plh), _TN))
    for b in range(nrow):
        y_ref[b] = jnp.concatenate([y[b * nh + h] for h in range(nh)], axis=-1)


def _scan_call(rwpack, nh, hd, chunk):
    bsz, seq, w6 = rwpack.shape
    rwid = nh * hd
    nrow = SCAN_ROWS if bsz % SCAN_ROWS == 0 else 1
    tri = jnp.tril(jnp.ones((chunk, chunk), F32))
    kern = functools.partial(_scan_kernel, nh=nh, hd=hd)
    return pl.pallas_call(
        kern,
        name="rwkv_scan",
        out_shape=jax.ShapeDtypeStruct((bsz, seq, rwid), F32),
        grid=(bsz // nrow, seq // chunk),
        in_specs=[pl.BlockSpec((nrow, chunk, w6), lambda b, s: (b, s, 0)),
                  _const_spec(tri.shape)],
        out_specs=pl.BlockSpec((nrow, chunk, rwid), lambda b, s: (b, s, 0)),
        scratch_shapes=[pltpu.VMEM((nrow * nh, hd, hd), F32)],
        compiler_params=_cparams(("arbitrary", "arbitrary")),
    )(rwpack, tri)


def _extract_max(s, iota, fill):
    m = jnp.max(s, axis=0, keepdims=True)
    idx = jnp.min(jnp.where(s == m, iota, fill), axis=0, keepdims=True)
    return m, idx, jnp.where(iota == idx, -jnp.inf, s)


def _topk_cols(s, k):
    n = s.shape[0]
    iota = lax.broadcasted_iota(jnp.int32, s.shape, 0)
    vals, idxs = [], []
    for _ in range(k):
        m, idx, s = _extract_max(s, iota, n)
        vals.append(m)
        idxs.append(idx)
    return jnp.concatenate(vals, axis=0), jnp.concatenate(idxs, axis=0)


def _post_kernel(x_ref, y_ref, yconv_ref, gb_ref, mod_ref, wout_ref, lnxg_ref, lnxb_ref,
                 ln1g_ref, ln1b_ref, wq_ref, k1_ref, k2_ref, bd_ref,
                 x1_ref, h2_ref, e_ref, gate_ref, q_ref, *, dm, rw, hd, alpha, nkeys, topk):
    tm = x_ref.shape[1]
    nheads = k1_ref.shape[0]
    half = k1_ref.shape[2]
    x = x_ref[0]
    mod = mod_ref[0]
    gate1 = mod[:, 2 * dm:3 * dm]
    shift2 = mod[:, 3 * dm:4 * dm]
    scale2 = mod[:, 4 * dm:5 * dm]

    y = y_ref[0]
    bd = bd_ref[...]
    mu = _head_sums(y, bd) * (1.0 / hd)
    yc = y - mu
    var = _head_sums(yc * yc, bd) * (1.0 / hd)
    gn = yc * lax.rsqrt(var + GN_EPS)
    gb = gb_ref[0]
    y_rw = (gn * lnxg_ref[...] + lnxb_ref[...] + gb[:, rw:2 * rw]) * gb[:, 0:rw]
    ycat = jnp.concatenate([yconv_ref[0], y_rw], axis=-1)
    y1 = _dot(ycat.astype(BF16), wout_ref[...])
    x1 = _ln(alpha * x + gate1 * y1, LN_EPS) * ln1g_ref[...] + ln1b_ref[...]
    x1_ref[0] = x1
    h2 = _ln(x1, LN_EPS) * (1.0 + scale2) + shift2
    h2_ref[0] = h2
    q = _dot(h2.astype(BF16), wq_ref[...])
    for i in range(2 * nheads):
        q_ref[i] = q[:, i * half:(i + 1) * half]

    sub = 8
    assert topk == 2 * sub
    r16 = lax.broadcasted_iota(jnp.int32, (topk, LANES), 0)
    r8 = lax.broadcasted_iota(jnp.int32, (sub, LANES), 0)
    flat = jnp.concatenate([r16] + [a * topk + r8 for a in range(1, sub)] + [(sub + r8) * topk],
                           axis=0)
    ncand = topk * topk

    def route(h, c0):
        s1 = lax.dot_general(k1_ref[h], q_ref[2 * h, c0:c0 + LANES, :], (((1,), (1,)), ((), ())),
                             preferred_element_type=F32, precision=HIGHEST)
        s2 = lax.dot_general(k2_ref[h], q_ref[2 * h + 1, c0:c0 + LANES, :],
                             (((1,), (1,)), ((), ())),
                             preferred_element_type=F32, precision=HIGHEST)
        v1, i1 = _topk_cols(s1, topk)
        v2, i2 = _topk_cols(s2, topk)
        cand = jnp.concatenate([v1[0:1, :] + v2]
                               + [v1[a:a + 1, :] + v2[0:sub, :] for a in range(1, sub)]
                               + [v1[sub:, :] + v2[0:1, :]], axis=0)
        eall = jnp.concatenate([i1[0:1, :] * nkeys + i2]
                               + [i1[a:a + 1, :] * nkeys + i2[0:sub, :] for a in range(1, sub)]
                               + [i1[sub:, :] * nkeys + i2[0:1, :]], axis=0)
        scs, exs = [], []
        for _ in range(topk):
            m, idx, cand_next = _extract_max(cand, flat, ncand)
            exs.append(jnp.max(jnp.where(flat == idx, eall, -1), axis=0, keepdims=True))
            scs.append(m)
            cand = cand_next
        sc = jnp.concatenate(scs, axis=0)
        ex = jnp.concatenate(exs, axis=0)
        pexp = jnp.exp(sc - sc[0:1, :])
        gates = pexp / jnp.sum(pexp, axis=0, keepdims=True)
        base = pl.multiple_of(h * topk, topk)
        e_ref[pl.ds(base, topk), c0:c0 + LANES] = ex
        gate_ref[pl.ds(base, topk), c0:c0 + LANES] = gates

    def head_body(h, carry):
        for c0 in range(0, tm, LANES):
            route(h, c0)
        return carry

    lax.fori_loop(0, nheads, head_body, 0)


def _post_call(x, yscan, yconv, gb, mod3, w_out, lnxg, lnxb, ln1g, ln1b, wq, k1, k2, bd,
               hd, alpha, topk, tm):
    bsz, seq, dm = x.shape
    rw = yscan.shape[-1]
    cw = yconv.shape[-1]
    nheads, nkeys, half = k1.shape
    nslot = nheads * topk
    ntok = bsz * seq
    nst = seq // tm
    row = lambda a: a.reshape(1, -1)
    consts = [w_out, row(lnxg), row(lnxb), row(ln1g), row(ln1b), wq, k1, k2, bd]
    kern = functools.partial(_post_kernel, dm=dm, rw=rw, hd=hd, alpha=alpha, nkeys=nkeys,
                             topk=topk)
    tok = lambda w: pl.BlockSpec((1, tm, w), lambda b, s: (b, s, 0))
    slot = pl.BlockSpec((nslot, tm), lambda b, s: (0, b * nst + s))
    return pl.pallas_call(
        kern,
        name="post_route",
        out_shape=(jax.ShapeDtypeStruct((bsz, seq, dm), F32),
                   jax.ShapeDtypeStruct((bsz, seq, dm), F32),
                   jax.ShapeDtypeStruct((nslot, ntok), jnp.int32),
                   jax.ShapeDtypeStruct((nslot, ntok), F32)),
        grid=(bsz, nst),
        in_specs=[tok(dm), tok(rw), tok(cw), tok(2 * rw),
                  pl.BlockSpec((1, 1, mod3.shape[-1]), lambda b, s: (b, 0, 0))]
                 + [_const_spec(a.shape) for a in consts],
        out_specs=(tok(dm), tok(dm), slot, slot),
        scratch_shapes=[pltpu.VMEM((2 * nheads, tm, half), F32)],
        compiler_params=_cparams(("arbitrary", "arbitrary")),
    )(x, yscan, yconv, gb, mod3, *consts)


def _pack_kernel(t_ref, o_ref):
    t = t_ref[...]
    hw = t.shape[1] // 2
    hi = lax.bitcast_convert_type(t[:, 0:hw].astype(BF16).astype(F32), jnp.uint32)
    lo = lax.bitcast_convert_type(t[:, hw:].astype(BF16).astype(F32), jnp.uint32)
    o_ref[...] = (hi & jnp.uint32(0xFFFF0000)) | (lo >> 16)


def _pack_call(table):
    ne, dm = table.shape
    packed = pl.pallas_call(
        _pack_kernel,
        name="peer_pack",
        out_shape=jax.ShapeDtypeStruct((ne, dm // 2), jnp.uint32),
        grid=(ne // PACK_ROWS,),
        in_specs=[pl.BlockSpec((PACK_ROWS, dm), lambda i: (i, 0))],
        out_specs=pl.BlockSpec((PACK_ROWS, dm // 2), lambda i: (i, 0)),
        compiler_params=_cparams(("arbitrary",)),
    )(table)
    return packed.reshape(ne * (dm // (2 * LANES)), LANES)


def _unpack(w):
    hi = lax.bitcast_convert_type(w & jnp.uint32(0xFFFF0000), F32)
    lo = lax.bitcast_convert_type(w << 16, F32)
    return hi, lo


def _gather_pair(tbl_ref, e_ref, off, nj):
    rows = [tbl_ref[pl.ds(pl.multiple_of(e_ref[off + d], nj), nj), :] for d in range(2)]
    return jnp.concatenate(rows, axis=0)


def _stack_chunks(row, first, nj):
    sub = lax.broadcasted_iota(jnp.int32, (2 * nj, LANES), 0) % nj
    chunk = lambda c: jnp.broadcast_to(row[:, c * LANES:(c + 1) * LANES], (2 * nj, LANES))
    out = chunk(first)
    for q in range(1, nj):
        out = jnp.where(sub == q, chunk(first + q), out)
    return out


def _peer_u_kernel(e_ref, h_ref, tbl_ref, z_ref, p0_ref, p1_ref, *, nslot, nj):
    tm = h_ref.shape[0]
    rows = 2 * nj
    lane = lax.broadcasted_iota(jnp.int32, (rows, LANES), 1)
    z_ref[...] = jnp.zeros(z_ref.shape, F32)

    @pl.when(pl.program_id(0) == 0)
    def _():
        p1_ref[...] = jnp.zeros(p1_ref.shape, F32)

    def reduce_rows(p_ref, row, tcol):
        rs = jnp.sum(p_ref[row:row + rows, :], axis=1, keepdims=True)
        z_ref[row:row + rows, :] = jnp.where(lane == tcol, rs, z_ref[row:row + rows, :])

    def one_token(t, p_ref, q_ref):
        hrow = h_ref[pl.ds(t, 1), :]
        h_hi = _stack_chunks(hrow, 0, nj)
        h_lo = _stack_chunks(hrow, nj, nj)
        ev = e_ref.at[pl.ds(t * nslot, nslot)]
        for i in range(0, nslot, 2):
            hi, lo = _unpack(_gather_pair(tbl_ref, ev, i, nj))
            p_ref[i * nj:i * nj + rows, :] = hi * h_hi + lo * h_lo
            reduce_rows(q_ref, i * nj, t - 1)

    def tok_body(i, carry):
        for d in range(0, U_TOKENS_PER_TRIP, 2):
            one_token(U_TOKENS_PER_TRIP * i + d, p0_ref, p1_ref)
            one_token(U_TOKENS_PER_TRIP * i + d + 1, p1_ref, p0_ref)
        return carry

    lax.fori_loop(0, tm // U_TOKENS_PER_TRIP, tok_body, 0)
    for r in range(0, nslot * nj, rows):
        reduce_rows(p1_ref, r, tm - 1)


def _peer_u_call(experts, h2, tbl, tm):
    ntok, dm = h2.shape
    nslot = experts.shape[0] // ntok
    nj = dm // (2 * LANES)
    kern = functools.partial(_peer_u_kernel, nslot=nslot, nj=nj)
    return pl.pallas_call(
        kern,
        name="peer_u",
        out_shape=jax.ShapeDtypeStruct((nslot * nj, ntok), F32),
        grid=(ntok // tm,),
        in_specs=[pl.BlockSpec((tm * nslot,), lambda i: (i,), memory_space=pltpu.SMEM),
                  pl.BlockSpec((tm, dm), lambda i: (i, 0)),
                  _const_spec(tbl.shape)],
        out_specs=pl.BlockSpec((nslot * nj, tm), lambda i: (0, i)),
        scratch_shapes=[pltpu.VMEM((nslot * nj, LANES), F32),
                        pltpu.VMEM((nslot * nj, LANES), F32)],
        compiler_params=_cparams(("arbitrary",)),
    )(experts, h2, tbl)


def _act_kernel(z_ref, sel_ref, gate_ref, a_ref):
    zp = z_ref[...]
    zh = zp.astype(BF16)
    zl = (zp - zh.astype(F32)).astype(BF16)
    sel = sel_ref[...]
    z = _dot(sel, zh) + _dot(sel, zl)
    a = 0.5 * z * (1.0 + lax.erf(z * (1.0 / math.sqrt(2.0)))) * gate_ref[...]
    a_ref[...] = a.T


def _act_call(zpart, gates, tm):
    nslot, ntok = gates.shape
    nj = zpart.shape[0] // nslot
    sel = (jnp.arange(nslot * nj)[None, :] // nj == jnp.arange(nslot)[:, None]).astype(BF16)
    return pl.pallas_call(
        _act_kernel,
        name="peer_act",
        out_shape=jax.ShapeDtypeStruct((ntok, nslot), F32),
        grid=(ntok // tm,),
        in_specs=[pl.BlockSpec((nslot * nj, tm), lambda i: (0, i)),
                  _const_spec(sel.shape),
                  pl.BlockSpec((nslot, tm), lambda i: (0, i))],
        out_specs=pl.BlockSpec((tm, nslot), lambda i: (i, 0)),
        compiler_params=_cparams(("arbitrary",)),
    )(zpart, sel, gates)


def _peer_v_kernel(e_ref, act_ref, tbl_ref, x_ref, mod_ref, g_ref, b_ref, o_ref,
                   y_ref, w0_ref, w1_ref, *, nslot, nj, nacc, alpha):
    tm, dm = o_ref.shape
    rows = 2 * nj
    upper = lax.broadcasted_iota(jnp.int32, (rows, LANES), 0) >= nj

    def expand(t, w_ref):
        w_ref[...] = jnp.broadcast_to(act_ref[pl.ds(t, 1), :], (nslot, nslot)).T

    def one_token(t, w_ref, wn_ref):
        expand(jnp.minimum(t + 1, tm - 1), wn_ref)
        ev = e_ref.at[pl.ds(t * nslot, nslot)]
        accs = [jnp.zeros((rows, LANES), F32) for _ in range(2 * nacc)]
        for i in range(0, nslot, 2):
            hi, lo = _unpack(_gather_pair(tbl_ref, ev, i, nj))
            a = jnp.where(upper, w_ref[i + 1:i + 2, :], w_ref[i:i + 1, :])
            j = (i // 2) % nacc
            accs[2 * j] = accs[2 * j] + a * hi
            accs[2 * j + 1] = accs[2 * j + 1] + a * lo
        acc_hi = sum(accs[2::2], accs[0])
        acc_lo = sum(accs[3::2], accs[1])
        y_ref[t] = jnp.concatenate([acc_hi[0:nj] + acc_hi[nj:], acc_lo[0:nj] + acc_lo[nj:]],
                                   axis=0)

    expand(0, w0_ref)

    def tok_body(i, carry):
        for d in range(0, V_TOKENS_PER_TRIP, 2):
            one_token(V_TOKENS_PER_TRIP * i + d, w0_ref, w1_ref)
            one_token(V_TOKENS_PER_TRIP * i + d + 1, w1_ref, w0_ref)
        return carry

    lax.fori_loop(0, tm // V_TOKENS_PER_TRIP, tok_body, 0)

    gate2 = mod_ref[0][:, 5 * dm:6 * dm]
    for r in range(0, tm, SUBLANES):
        y2 = jnp.concatenate([y_ref[r:r + SUBLANES, c, :] for c in range(2 * nj)], axis=1)
        z = alpha * x_ref[r:r + SUBLANES, :] + gate2 * y2
        o_ref[r:r + SUBLANES, :] = _ln(z, LN_EPS) * g_ref[...] + b_ref[...]


def _peer_v_call(experts, act, tbl, x1, mod3, g, b, alpha, seq, tm):
    ntok, nslot = act.shape
    dm = x1.shape[-1]
    nj = dm // (2 * LANES)
    assert nslot == LANES and seq % tm == 0
    kern = functools.partial(_peer_v_kernel, nslot=nslot, nj=nj, nacc=2, alpha=alpha)
    tok = pl.BlockSpec((tm, dm), lambda i: (i, 0))
    return pl.pallas_call(
        kern,
        name="peer_v",
        out_shape=jax.ShapeDtypeStruct((ntok, dm), F32),
        grid=(ntok // tm,),
        in_specs=[pl.BlockSpec((tm * nslot,), lambda i: (i,), memory_space=pltpu.SMEM),
                  pl.BlockSpec((tm, nslot), lambda i: (i, 0)),
                  _const_spec(tbl.shape),
                  tok,
                  pl.BlockSpec((1, 1, mod3.shape[-1]), lambda i: (i * tm // seq, 0, 0)),
                  _const_spec((1, dm)), _const_spec((1, dm))],
        out_specs=tok,
        scratch_shapes=[pltpu.VMEM((tm, 2 * nj, LANES), F32),
                        pltpu.VMEM((nslot, LANES), F32), pltpu.VMEM((nslot, LANES), F32)],
        compiler_params=_cparams(("arbitrary",)),
    )(experts, act, tbl, x1, mod3, g.reshape(1, -1), b.reshape(1, -1))


def kernel(x, c, cond_w, cond_b, w_in, mu_shift, conv_w, conv_b, conv_ln_g, conv_ln_b, rw_w0, rw_w2, rw_a0, rw_a2, rw_g2, rw_kk, rw_ka, rw_rk, rw_lnx_g, rw_lnx_b, w_out, ln1_g, ln1_b, peer_wq, peer_k1, peer_k2, peer_u, peer_v, ln2_g, ln2_b):
    bsz, seq, dm = x.shape
    nh, hd = rw_rk.shape
    rw = nh * hd
    lora_w, lora_a = rw_w2.shape[0], rw_a2.shape[0]
    assert lora_w == lora_a == LANES // 2 and rw_g2.shape[0] == LANES
    topk = PEER_TOPK
    alpha = (2.0 * DEPTH) ** 0.25
    tm = min(ROW_TILE, seq)
    chunk = min(SCAN_CHUNK, seq)

    mod3 = _mod_call(c, cond_w, cond_b).reshape(bsz, 1, -1)

    head_id = jnp.arange(rw) // hd
    bd = (head_id[:, None] == head_id[None, :]).astype(BF16)
    wa2 = jnp.zeros((LANES, 2 * rw), F32)
    wa2 = wa2.at[0:lora_w, 0:rw].set(rw_w2).at[lora_w:, rw:].set(rw_a2)

    rwpack, yconv, gb = _front_call(
        x, mod3, w_in.astype(BF16), mu_shift, conv_w, conv_b, conv_ln_g, conv_ln_b, rw_w0,
        rw_a0, wa2, rw_g2, rw_kk, rw_ka, rw_rk.reshape(-1), bd, min(FRONT_TILE, seq))
    yscan = _scan_call(rwpack, nh, hd, chunk)
    x1, h2, experts, gates = _post_call(
        x, yscan, yconv, gb, mod3, w_out.astype(BF16), rw_lnx_g, rw_lnx_b, ln1_g, ln1_b,
        peer_wq.astype(BF16), peer_k1, peer_k2, bd, hd, alpha, topk, tm)

    ntok = bsz * seq
    ptm = min(PEER_TOK, seq)
    nj = dm // (2 * LANES)
    experts_t = (experts * nj).T.reshape(-1)
    zpart = _peer_u_call(experts_t, h2.reshape(ntok, dm), _pack_call(peer_u), ptm)
    act = _act_call(zpart, gates, ptm)
    out = _peer_v_call(experts_t, act, _pack_call(peer_v), x1.reshape(ntok, dm), mod3,
                       ln2_g, ln2_b, alpha, seq, ptm)
    return out.reshape(bsz, seq, dm)
```

```python
import functools
import math

import jax
import jax.numpy as jnp
from jax import lax
from jax.experimental import pallas as pl
from jax.experimental.pallas import tpu as pltpu

F32 = jnp.float32
BF16 = jnp.bfloat16
HIGHEST = lax.Precision.HIGHEST

LN_EPS = 1e-5
GN_EPS = 64e-5
PEER_TOPK = 16
DEPTH = 1
LANES = 128
SUBLANES = 8
ROW_TILE = 256
FRONT_TILE = 256
ROUTE_HEADS_PER_TRIP = 4
SCAN_CHUNK = 64
SCAN_ROWS = 4
PEER_TOK = 128
U_TOKENS_PER_TRIP = 8
V_TOKENS_PER_TRIP = 2
PACK_ROWS = 512
VMEM_LIMIT = 56 * 1024 * 1024


def _cparams(sem):
    return pltpu.CompilerParams(dimension_semantics=sem, vmem_limit_bytes=VMEM_LIMIT)


def _dot(a, b, precision=None):
    return jnp.dot(a, b, preferred_element_type=F32, precision=precision)


def _sigmoid(x):
    return 1.0 / (1.0 + jnp.exp(-x))


def _ln(x, eps):
    mu = jnp.mean(x, axis=-1, keepdims=True)
    xc = x - mu
    var = jnp.mean(xc * xc, axis=-1, keepdims=True)
    return xc * lax.rsqrt(var + eps)


def _head_sums(x, bd):
    hi = x.astype(BF16)
    lo = (x - hi.astype(F32)).astype(BF16)
    return _dot(hi, bd) + _dot(lo, bd)


def _const_spec(shape):
    nd = len(shape)
    return pl.BlockSpec(shape, lambda *_: (0,) * nd, pipeline_mode=pl.Buffered(1))


def _mod_kernel(c_ref, w_ref, b_ref, o_ref):
    c = c_ref[...]
    o_ref[...] = _dot(c * _sigmoid(c), w_ref[...], HIGHEST) + b_ref[...]


def _mod_call(c, cond_w, cond_b):
    bsz, dm = c.shape
    nblk = cond_w.shape[1] // dm
    return pl.pallas_call(
        _mod_kernel,
        name="mod",
        out_shape=jax.ShapeDtypeStruct((bsz, nblk * dm), F32),
        grid=(nblk,),
        in_specs=[pl.BlockSpec((bsz, dm), lambda j: (0, 0)),
                  pl.BlockSpec((dm, dm), lambda j: (0, j)),
                  pl.BlockSpec((1, dm), lambda j: (0, j))],
        out_specs=pl.BlockSpec((bsz, dm), lambda j: (0, j)),
        compiler_params=_cparams(("arbitrary",)),
    )(c, cond_w, cond_b.reshape(1, -1))


def _front_kernel(x_ref, mod_ref, win_ref, mu_ref, convw_ref, convb_ref, cg_ref, cb_ref,
                  w0_ref, a0_ref, wa2_ref, g2_ref, kkw_ref, kaw_ref, rk_ref, bd_ref,
                  rw_ref, yconv_ref, gb_ref, ubuf_ref, prev_ref, *, dm, cw, rw, ksize, halo):
    tm = x_ref.shape[1]
    first = pl.program_id(1) == 0

    @pl.when(first)
    def _():
        ubuf_ref[0:halo, :] = jnp.zeros((halo, cw), F32)
        prev_ref[...] = jnp.zeros(prev_ref.shape, F32)

    x = x_ref[0]
    mod = mod_ref[0]
    shift1 = mod[:, 0:dm]
    scale1 = mod[:, dm:2 * dm]
    h = _ln(x, LN_EPS) * (1.0 + scale1) + shift1
    p = _dot(h.astype(BF16), win_ref[...])

    u = p[:, 0:cw] * _sigmoid(p[:, cw:2 * cw])
    ubuf_ref[halo:halo + tm, :] = u
    acc = jnp.zeros((tm, cw), F32) + convb_ref[...]
    off = halo - (ksize - 1)
    ub = ubuf_ref[...]
    nrow = tm + halo
    for b in range(SUBLANES):
        xb = ub if b == 0 else pltpu.roll(ub, nrow - b, 0)
        for j in range(ksize):
            if (off + j) % SUBLANES == b:
                a8 = off + j - b
                acc = acc + convw_ref[j:j + 1, :] * xb[a8:a8 + tm, :]
    ubuf_ref[0:halo, :] = ubuf_ref[tm:tm + halo, :]
    yc = _ln(acc, LN_EPS) * cg_ref[...] + cb_ref[...]
    yconv_ref[0] = yc * _sigmoid(yc)

    prw = p[:, 2 * cw:]
    rolled = pltpu.roll(prw, 1, 0)
    row = lax.broadcasted_iota(jnp.int32, prw.shape, 0)
    p_prev = jnp.where(row == 0, prev_ref[0:1, :], rolled)
    prev_ref[0:1, :] = prw[tm - 1:tm, :]
    xm = prw + (p_prev - prw) * mu_ref[...]
    r = xm[:, 0:rw]
    k = xm[:, rw:2 * rw]
    v = xm[:, 2 * rw:3 * rw]
    wa = xm[:, 3 * rw:3 * rw + LANES]
    gd = xm[:, 3 * rw + LANES:3 * rw + 2 * LANES]
    lane = lax.broadcasted_iota(jnp.int32, wa.shape, 1)
    wa = jnp.where(lane < LANES // 2, jnp.tanh(wa), wa)
    t12 = _dot(wa, wa2_ref[...], HIGHEST)
    g = _dot(_sigmoid(gd), g2_ref[...], HIGHEST)
    y = w0_ref[...] + t12[:, 0:rw]
    sp = jnp.maximum(-y, 0.0) + jnp.log(1.0 + jnp.exp(-jnp.abs(y)))
    lw = -jnp.exp(-sp - 0.5)
    a = _sigmoid(a0_ref[...] + t12[:, rw:2 * rw])
    kk = k * kkw_ref[...]
    ss = _head_sums(kk * kk, bd_ref[...])
    kk = kk / jnp.maximum(jnp.sqrt(ss), 1e-12)
    kmod = k * (1.0 + (a - 1.0) * kaw_ref[...])
    bonus = _head_sums(r * kmod * rk_ref[...], bd_ref[...]) * v
    rw_ref[0, :, 0:rw] = r
    rw_ref[0, :, rw:2 * rw] = kmod
    rw_ref[0, :, 2 * rw:3 * rw] = v
    rw_ref[0, :, 3 * rw:4 * rw] = kk
    rw_ref[0, :, 4 * rw:5 * rw] = kk * a
    rw_ref[0, :, 5 * rw:6 * rw] = lw
    gb_ref[0, :, 0:rw] = g
    gb_ref[0, :, rw:2 * rw] = bonus


def _front_call(x, mod3, w_in, mu_shift, conv_w, conv_b, cg, cb, w0, a0, wa2, g2, kkw, kaw,
                rk, bd, tm):
    bsz, seq, dm = x.shape
    ksize, cw = conv_w.shape
    rw = w0.shape[-1]
    halo = 32
    assert ksize - 1 <= halo <= tm
    row = lambda a: a.reshape(1, -1)
    consts = [w_in, row(mu_shift), conv_w, row(conv_b), row(cg), row(cb), row(w0), row(a0),
              wa2, g2, row(kkw), row(kaw), row(rk), bd]
    kern = functools.partial(_front_kernel, dm=dm, cw=cw, rw=rw, ksize=ksize, halo=halo)
    return pl.pallas_call(
        kern,
        name="front",
        out_shape=(jax.ShapeDtypeStruct((bsz, seq, 6 * rw), F32),
                   jax.ShapeDtypeStruct((bsz, seq, cw), F32),
                   jax.ShapeDtypeStruct((bsz, seq, 2 * rw), F32)),
        grid=(bsz, seq // tm),
        in_specs=[pl.BlockSpec((1, tm, dm), lambda b, s: (b, s, 0)),
                  pl.BlockSpec((1, 1, mod3.shape[-1]), lambda b, s: (b, 0, 0))]
                 + [_const_spec(a.shape) for a in consts],
        out_specs=(pl.BlockSpec((1, tm, 6 * rw), lambda b, s: (b, s, 0)),
                   pl.BlockSpec((1, tm, cw), lambda b, s: (b, s, 0)),
                   pl.BlockSpec((1, tm, 2 * rw), lambda b, s: (b, s, 0))),
        scratch_shapes=[pltpu.VMEM((tm + halo, cw), F32),
                        pltpu.VMEM((8, w_in.shape[1] - 2 * cw), F32)],
        compiler_params=_cparams(("arbitrary", "arbitrary")),
    )(x, mod3, *consts)


SCAN_PASSES = 1
_NN = (((2,), (1,)), ((0,), (0,)))
_NT = (((2,), (2,)), ((0,), (0,)))
_TN = (((1,), (1,)), ((0,), (0,)))


def _split(x, passes=3):
    hi = x.astype(BF16)
    if passes == 1:
        return hi, None
    return hi, (x - hi.astype(F32)).astype(BF16)


def _mm3(a, b, dims):
    (ah, al), (bh, bl) = a, b
    dg = lambda p, q: lax.dot_general(p, q, dims, preferred_element_type=F32)
    out = dg(ah, bh)
    if bl is not None:
        out = out + dg(ah, bl)
    if al is not None:
        out = out + dg(al, bh)
    return out


def _scan_kernel(rw_ref, tri_ref, y_ref, state_ref, *, nh, hd):
    L = rw_ref.shape[1]
    rwid = nh * hd

    @pl.when(pl.program_id(1) == 0)
    def _():
        state_ref[...] = jnp.zeros(state_ref.shape, F32)

    nrow = rw_ref.shape[0]
    tri = tri_ref[...]
    parts = [[] for _ in range(6)]
    for b in range(nrow):
        blk = rw_ref[b]
        lw = blk[:, 5 * rwid:6 * rwid]
        c = _dot(tri, lw, HIGHEST)
        e_pos = jnp.exp(c)
        e_neg = jnp.exp(-c)
        parts[0].append(blk[:, 0:rwid] * e_pos)
        parts[1].append(blk[:, rwid:2 * rwid] * e_neg)
        parts[2].append(blk[:, 3 * rwid:4 * rwid] * jnp.exp(c - lw))
        parts[3].append(blk[:, 4 * rwid:5 * rwid] * e_neg)
        parts[4].append(blk[:, 2 * rwid:3 * rwid])
        parts[5].append(e_pos[L - 1:L, :])

    def heads(ts):
        return jnp.stack([t[:, h * hd:(h + 1) * hd] for t in ts for h in range(nh)], axis=0)

    rt, kh, kt, bh, v, plh = (heads(p) for p in parts)
    s0 = state_ref[...]

    strict = (tri - jnp.eye(L, dtype=F32))[None]
    incl = tri[None]
    sp = functools.partial(_split, passes=SCAN_PASSES)
    kt_s, bh_s, kh_s, rt_s, v_s, s0_s = (sp(t) for t in (kt, bh, kh, rt, v, s0))
    t_bb = _mm3(kt_s, bh_s, _NT) * strict
    t_bk = _mm3(kt_s, kh_s, _NT) * strict
    g_rb = _mm3(rt_s, bh_s, _NT) * incl
    g_rk = _mm3(rt_s, kh_s, _NT) * incl

    w = _mm3(kt_s, s0_s, _NT) + _mm3(sp(t_bk), v_s, _NN)
    tp_s = sp(t_bb)
    w = w - _mm3(tp_s, sp(w), _NN)
    n = 2
    while n < L:
        tp_s = sp(_mm3(tp_s, tp_s, _NN))
        w = w + _mm3(tp_s, sp(w), _NN)
        n *= 2
    u_s = sp(w)
    y = (_mm3(rt_s, s0_s, _NT) - _mm3(sp(g_rb), u_s, _NN)
         + _mm3(sp(g_rk), v_s, _NN))
    state_ref[...] = (s0 * plh + _mm3(v_s, sp(kh * plh), _TN)
                      - _mm3(u_s, sp(bh * plh), _TN))
    for b in range(nrow):
        y_ref[b] = jnp.concatenate([y[b * nh + h] for h in range(nh)], axis=-1)


def _scan_call(rwpack, nh, hd, chunk):
    bsz, seq, w6 = rwpack.shape
    rwid = nh * hd
    nrow = SCAN_ROWS if bsz % SCAN_ROWS == 0 else 1
    tri = jnp.tril(jnp.ones((chunk, chunk), F32))
    kern = functools.partial(_scan_kernel, nh=nh, hd=hd)
    return pl.pallas_call(
        kern,
        name="rwkv_scan",
        out_shape=jax.ShapeDtypeStruct((bsz, seq, rwid), F32),
        grid=(bsz // nrow, seq // chunk),
        in_specs=[pl.BlockSpec((nrow, chunk, w6), lambda b, s: (b, s, 0)),
                  _const_spec(tri.shape)],
        out_specs=pl.BlockSpec((nrow, chunk, rwid), lambda b, s: (b, s, 0)),
        scratch_shapes=[pltpu.VMEM((nrow * nh, hd, hd), F32)],
        compiler_params=_cparams(("arbitrary", "arbitrary")),
    )(rwpack, tri)


def _extract_max(s, iota, fill, remove=True):
    m = jnp.max(s, axis=0, keepdims=True)
    idx = jnp.min(jnp.where(s == m, iota, fill), axis=0, keepdims=True)
    return m, idx, (jnp.where(iota == idx, -jnp.inf, s) if remove else s)


def _topk_cols(s, iota, fill, k):
    vals, idxs = [], []
    for r in range(k):
        m, idx, s = _extract_max(s, iota, fill, remove=r + 1 < k)
        vals.append(m)
        idxs.append(idx)
    return jnp.concatenate(vals, axis=0), jnp.concatenate(idxs, axis=0)


def _take_rows(table, idx):
    out = jnp.zeros(idx.shape, table.dtype)
    for a in range(table.shape[0]):
        out = jnp.where(idx == a, table[a:a + 1, :], out)
    return out


def _post_kernel(x_ref, y_ref, yconv_ref, gb_ref, mod_ref, wout_ref, lnxg_ref, lnxb_ref,
                 ln1g_ref, ln1b_ref, wq_ref, k1_ref, k2_ref, bd_ref,
                 x1_ref, h2_ref, e_ref, gate_ref, q_ref, *, dm, rw, hd, alpha, nkeys, topk):
    tm = x_ref.shape[1]
    nheads = k1_ref.shape[0]
    half = k1_ref.shape[2]
    x = x_ref[0]
    mod = mod_ref[0]
    gate1 = mod[:, 2 * dm:3 * dm]
    shift2 = mod[:, 3 * dm:4 * dm]
    scale2 = mod[:, 4 * dm:5 * dm]

    y = y_ref[0]
    bd = bd_ref[...]
    mu = _head_sums(y, bd) * (1.0 / hd)
    yc = y - mu
    var = _head_sums(yc * yc, bd) * (1.0 / hd)
    gn = yc * lax.rsqrt(var + GN_EPS)
    gb = gb_ref[0]
    y_rw = (gn * lnxg_ref[...] + lnxb_ref[...] + gb[:, rw:2 * rw]) * gb[:, 0:rw]
    ycat = jnp.concatenate([yconv_ref[0], y_rw], axis=-1)
    y1 = _dot(ycat.astype(BF16), wout_ref[...])
    x1 = _ln(alpha * x + gate1 * y1, LN_EPS) * ln1g_ref[...] + ln1b_ref[...]
    x1_ref[0] = x1
    h2 = _ln(x1, LN_EPS) * (1.0 + scale2) + shift2
    h2_ref[0] = h2
    q = _dot(h2.astype(BF16), wq_ref[...])
    for i in range(2 * nheads):
        q_ref[i] = q[:, i * half:(i + 1) * half]

    sub = 8
    assert topk == 2 * sub
    r16 = lax.broadcasted_iota(jnp.int32, (topk, LANES), 0)
    r8 = lax.broadcasted_iota(jnp.int32, (sub, LANES), 0)
    flat = jnp.concatenate([r16] + [a * topk + r8 for a in range(1, sub)] + [(sub + r8) * topk],
                           axis=0)
    ncand = topk * topk
    shift = topk.bit_length() - 1
    key_iota = lax.broadcasted_iota(jnp.int32, (nkeys, LANES), 0)

    def route(h, c0):
        s1 = lax.dot_general(k1_ref[h], q_ref[2 * h, c0:c0 + LANES, :], (((1,), (1,)), ((), ())),
                             preferred_element_type=F32, precision=HIGHEST)
        s2 = lax.dot_general(k2_ref[h], q_ref[2 * h + 1, c0:c0 + LANES, :],
                             (((1,), (1,)), ((), ())),
                             preferred_element_type=F32, precision=HIGHEST)
        v1, i1 = _topk_cols(s1, key_iota, nkeys, topk)
        v2, i2 = _topk_cols(s2, key_iota, nkeys, topk)
        cand = jnp.concatenate([v1[0:1, :] + v2]
                               + [v1[a:a + 1, :] + v2[0:sub, :] for a in range(1, sub)]
                               + [v1[sub:, :] + v2[0:1, :]], axis=0)
        sc, pick = _topk_cols(cand, flat, ncand, topk)
        ex = (_take_rows(i1, lax.shift_right_logical(pick, shift)) * nkeys
              + _take_rows(i2, pick & (topk - 1)))
        pexp = jnp.exp(sc - sc[0:1, :])
        gates = pexp / jnp.sum(pexp, axis=0, keepdims=True)
        base = pl.multiple_of(h * topk, topk)
        e_ref[pl.ds(base, topk), c0:c0 + LANES] = ex
        gate_ref[pl.ds(base, topk), c0:c0 + LANES] = gates

    def head_body(hp, carry):
        for dh in range(ROUTE_HEADS_PER_TRIP):
            for c0 in range(0, tm, LANES):
                route(ROUTE_HEADS_PER_TRIP * hp + dh, c0)
        return carry

    lax.fori_loop(0, nheads // ROUTE_HEADS_PER_TRIP, head_body, 0)


def _post_call(x, yscan, yconv, gb, mod3, w_out, lnxg, lnxb, ln1g, ln1b, wq, k1, k2, bd,
               hd, alpha, topk, tm):
    bsz, seq, dm = x.shape
    rw = yscan.shape[-1]
    cw = yconv.shape[-1]
    nheads, nkeys, half = k1.shape
    nslot = nheads * topk
    ntok = bsz * seq
    nst = seq // tm
    row = lambda a: a.reshape(1, -1)
    consts = [w_out, row(lnxg), row(lnxb), row(ln1g), row(ln1b), wq, k1, k2, bd]
    kern = functools.partial(_post_kernel, dm=dm, rw=rw, hd=hd, alpha=alpha, nkeys=nkeys,
                             topk=topk)
    tok = lambda w: pl.BlockSpec((1, tm, w), lambda b, s: (b, s, 0))
    slot = pl.BlockSpec((nslot, tm), lambda b, s: (0, b * nst + s))
    return pl.pallas_call(
        kern,
        name="post_route",
        out_shape=(jax.ShapeDtypeStruct((bsz, seq, dm), F32),
                   jax.ShapeDtypeStruct((bsz, seq, dm), F32),
                   jax.ShapeDtypeStruct((nslot, ntok), jnp.int32),
                   jax.ShapeDtypeStruct((nslot, ntok), F32)),
        grid=(bsz, nst),
        in_specs=[tok(dm), tok(rw), tok(cw), tok(2 * rw),
                  pl.BlockSpec((1, 1, mod3.shape[-1]), lambda b, s: (b, 0, 0))]
                 + [_const_spec(a.shape) for a in consts],
        out_specs=(tok(dm), tok(dm), slot, slot),
        scratch_shapes=[pltpu.VMEM((2 * nheads, tm, half), F32)],
        compiler_params=_cparams(("arbitrary", "arbitrary")),
    )(x, yscan, yconv, gb, mod3, *consts)


def _pack_kernel(t_ref, o_ref):
    t = t_ref[...]
    hw = t.shape[1] // 2
    hi = lax.bitcast_convert_type(t[:, 0:hw].astype(BF16).astype(F32), jnp.uint32)
    lo = lax.bitcast_convert_type(t[:, hw:].astype(BF16).astype(F32), jnp.uint32)
    o_ref[...] = (hi & jnp.uint32(0xFFFF0000)) | (lo >> 16)


def _pack_call(table):
    ne, dm = table.shape
    packed = pl.pallas_call(
        _pack_kernel,
        name="peer_pack",
        out_shape=jax.ShapeDtypeStruct((ne, dm // 2), jnp.uint32),
        grid=(ne // PACK_ROWS,),
        in_specs=[pl.BlockSpec((PACK_ROWS, dm), lambda i: (i, 0))],
        out_specs=pl.BlockSpec((PACK_ROWS, dm // 2), lambda i: (i, 0)),
        compiler_params=_cparams(("arbitrary",)),
    )(table)
    return packed.reshape(ne * (dm // (2 * LANES)), LANES)


def _unpack(w):
    hi = lax.bitcast_convert_type(w & jnp.uint32(0xFFFF0000), F32)
    lo = lax.bitcast_convert_type(w << 16, F32)
    return hi, lo


def _gather_pair(tbl_ref, e_ref, off, nj):
    rows = [tbl_ref[pl.ds(pl.multiple_of(e_ref[off + d], nj), nj), :] for d in range(2)]
    return jnp.concatenate(rows, axis=0)


def _stack_chunks(row, first, nj):
    sub = lax.broadcasted_iota(jnp.int32, (2 * nj, LANES), 0) % nj
    chunk = lambda c: jnp.broadcast_to(row[:, c * LANES:(c + 1) * LANES], (2 * nj, LANES))
    out = chunk(first)
    for q in range(1, nj):
        out = jnp.where(sub == q, chunk(first + q), out)
    return out


def _peer_u_kernel(e_ref, h_ref, tbl_ref, z_ref, p0_ref, p1_ref, *, nslot, nj):
    tm = h_ref.shape[0]
    rows = 2 * nj
    lane = lax.broadcasted_iota(jnp.int32, (rows, LANES), 1)
    z_ref[...] = jnp.zeros(z_ref.shape, F32)

    @pl.when(pl.program_id(0) == 0)
    def _():
        p1_ref[...] = jnp.zeros(p1_ref.shape, F32)

    def reduce_rows(p_ref, row, tcol):
        rs = jnp.sum(p_ref[row:row + rows, :], axis=1, keepdims=True)
        z_ref[row:row + rows, :] = jnp.where(lane == tcol, rs, z_ref[row:row + rows, :])

    def one_token(t, p_ref, q_ref):
        hrow = h_ref[pl.ds(t, 1), :]
        h_hi = _stack_chunks(hrow, 0, nj)
        h_lo = _stack_chunks(hrow, nj, nj)
        ev = e_ref.at[pl.ds(t * nslot, nslot)]
        for i in range(0, nslot, 2):
            hi, lo = _unpack(_gather_pair(tbl_ref, ev, i, nj))
            p_ref[i * nj:i * nj + rows, :] = hi * h_hi + lo * h_lo
            reduce_rows(q_ref, i * nj, t - 1)

    def tok_body(i, carry):
        for d in range(0, U_TOKENS_PER_TRIP, 2):
            one_token(U_TOKENS_PER_TRIP * i + d, p0_ref, p1_ref)
            one_token(U_TOKENS_PER_TRIP * i + d + 1, p1_ref, p0_ref)
        return carry

    lax.fori_loop(0, tm // U_TOKENS_PER_TRIP, tok_body, 0)
    for r in range(0, nslot * nj, rows):
        reduce_rows(p1_ref, r, tm - 1)


def _peer_u_call(experts, h2, tbl, tm):
    ntok, dm = h2.shape
    nslot = experts.shape[0] // ntok
    nj = dm // (2 * LANES)
    kern = functools.partial(_peer_u_kernel, nslot=nslot, nj=nj)
    return pl.pallas_call(
        kern,
        name="peer_u",
        out_shape=jax.ShapeDtypeStruct((nslot * nj, ntok), F32),
        grid=(ntok // tm,),
        in_specs=[pl.BlockSpec((tm * nslot,), lambda i: (i,), memory_space=pltpu.SMEM),
                  pl.BlockSpec((tm, dm), lambda i: (i, 0)),
                  _const_spec(tbl.shape)],
        out_specs=pl.BlockSpec((nslot * nj, tm), lambda i: (0, i)),
        scratch_shapes=[pltpu.VMEM((nslot * nj, LANES), F32),
                        pltpu.VMEM((nslot * nj, LANES), F32)],
        compiler_params=_cparams(("arbitrary",)),
    )(experts, h2, tbl)


def _act_kernel(z_ref, sel_ref, gate_ref, a_ref):
    zp = z_ref[...]
    zh = zp.astype(BF16)
    zl = (zp - zh.astype(F32)).astype(BF16)
    sel = sel_ref[...]
    z = _dot(sel, zh) + _dot(sel, zl)
    a = 0.5 * z * (1.0 + lax.erf(z * (1.0 / math.sqrt(2.0)))) * gate_ref[...]
    a_ref[...] = a.T


def _act_call(zpart, gates, tm):
    nslot, ntok = gates.shape
    nj = zpart.shape[0] // nslot
    sel = (jnp.arange(nslot * nj)[None, :] // nj == jnp.arange(nslot)[:, None]).astype(BF16)
    return pl.pallas_call(
        _act_kernel,
        name="peer_act",
        out_shape=jax.ShapeDtypeStruct((ntok, nslot), F32),
        grid=(ntok // tm,),
        in_specs=[pl.BlockSpec((nslot * nj, tm), lambda i: (0, i)),
                  _const_spec(sel.shape),
                  pl.BlockSpec((nslot, tm), lambda i: (0, i))],
        out_specs=pl.BlockSpec((tm, nslot), lambda i: (i, 0)),
        compiler_params=_cparams(("arbitrary",)),
    )(zpart, sel, gates)


def _peer_v_kernel(e_ref, act_ref, tbl_ref, x_ref, mod_ref, g_ref, b_ref, o_ref,
                   y_ref, w0_ref, w1_ref, *, nslot, nj, nacc, alpha):
    tm, dm = o_ref.shape
    rows = 2 * nj
    upper = lax.broadcasted_iota(jnp.int32, (rows, LANES), 0) >= nj

    def expand(t, w_ref):
        w_ref[...] = jnp.broadcast_to(act_ref[pl.ds(t, 1), :], (nslot, nslot)).T

    def one_token(t, w_ref, wn_ref):
        expand(jnp.minimum(t + 1, tm - 1), wn_ref)
        ev = e_ref.at[pl.ds(t * nslot, nslot)]
        accs = [jnp.zeros((rows, LANES), F32) for _ in range(2 * nacc)]
        for i in range(0, nslot, 2):
            hi, lo = _unpack(_gather_pair(tbl_ref, ev, i, nj))
            a = jnp.where(upper, w_ref[i + 1:i + 2, :], w_ref[i:i + 1, :])
            j = (i // 2) % nacc
            accs[2 * j] = accs[2 * j] + a * hi
            accs[2 * j + 1] = accs[2 * j + 1] + a * lo
        acc_hi = sum(accs[2::2], accs[0])
        acc_lo = sum(accs[3::2], accs[1])
        y_ref[t] = jnp.concatenate([acc_hi[0:nj] + acc_hi[nj:], acc_lo[0:nj] + acc_lo[nj:]],
                                   axis=0)

    expand(0, w0_ref)

    def tok_body(i, carry):
        for d in range(0, V_TOKENS_PER_TRIP, 2):
            one_token(V_TOKENS_PER_TRIP * i + d, w0_ref, w1_ref)
            one_token(V_TOKENS_PER_TRIP * i + d + 1, w1_ref, w0_ref)
        return carry

    lax.fori_loop(0, tm // V_TOKENS_PER_TRIP, tok_body, 0)

    gate2 = mod_ref[0][:, 5 * dm:6 * dm]
    for r in range(0, tm, SUBLANES):
        y2 = jnp.concatenate([y_ref[r:r + SUBLANES, c, :] for c in range(2 * nj)], axis=1)
        z = alpha * x_ref[r:r + SUBLANES, :] + gate2 * y2
        o_ref[r:r + SUBLANES, :] = _ln(z, LN_EPS) * g_ref[...] + b_ref[...]


def _peer_v_call(experts, act, tbl, x1, mod3, g, b, alpha, seq, tm):
    ntok, nslot = act.shape
    dm = x1.shape[-1]
    nj = dm // (2 * LANES)
    assert nslot == LANES and seq % tm == 0
    kern = functools.partial(_peer_v_kernel, nslot=nslot, nj=nj, nacc=2, alpha=alpha)
    tok = pl.BlockSpec((tm, dm), lambda i: (i, 0))
    return pl.pallas_call(
        kern,
        name="peer_v",
        out_shape=jax.ShapeDtypeStruct((ntok, dm), F32),
        grid=(ntok // tm,),
        in_specs=[pl.BlockSpec((tm * nslot,), lambda i: (i,), memory_space=pltpu.SMEM),
                  pl.BlockSpec((tm, nslot), lambda i: (i, 0)),
                  _const_spec(tbl.shape),
                  tok,
                  pl.BlockSpec((1, 1, mod3.shape[-1]), lambda i: (i * tm // seq, 0, 0)),
                  _const_spec((1, dm)), _const_spec((1, dm))],
        out_specs=tok,
        scratch_shapes=[pltpu.VMEM((tm, 2 * nj, LANES), F32),
                        pltpu.VMEM((nslot, LANES), F32), pltpu.VMEM((nslot, LANES), F32)],
        compiler_params=_cparams(("arbitrary",)),
    )(experts, act, tbl, x1, mod3, g.reshape(1, -1), b.reshape(1, -1))


def kernel(x, c, cond_w, cond_b, w_in, mu_shift, conv_w, conv_b, conv_ln_g, conv_ln_b, rw_w0, rw_w2, rw_a0, rw_a2, rw_g2, rw_kk, rw_ka, rw_rk, rw_lnx_g, rw_lnx_b, w_out, ln1_g, ln1_b, peer_wq, peer_k1, peer_k2, peer_u, peer_v, ln2_g, ln2_b):
    bsz, seq, dm = x.shape
    nh, hd = rw_rk.shape
    rw = nh * hd
    lora_w, lora_a = rw_w2.shape[0], rw_a2.shape[0]
    assert lora_w == lora_a == LANES // 2 and rw_g2.shape[0] == LANES
    topk = PEER_TOPK
    alpha = (2.0 * DEPTH) ** 0.25
    tm = min(ROW_TILE, seq)
    chunk = min(SCAN_CHUNK, seq)

    mod3 = _mod_call(c, cond_w, cond_b).reshape(bsz, 1, -1)

    head_id = jnp.arange(rw) // hd
    bd = (head_id[:, None] == head_id[None, :]).astype(BF16)
    wa2 = jnp.zeros((LANES, 2 * rw), F32)
    wa2 = wa2.at[0:lora_w, 0:rw].set(rw_w2).at[lora_w:, rw:].set(rw_a2)

    rwpack, yconv, gb = _front_call(
        x, mod3, w_in.astype(BF16), mu_shift, conv_w, conv_b, conv_ln_g, conv_ln_b, rw_w0,
        rw_a0, wa2, rw_g2, rw_kk, rw_ka, rw_rk.reshape(-1), bd, min(FRONT_TILE, seq))
    yscan = _scan_call(rwpack, nh, hd, chunk)
    x1, h2, experts, gates = _post_call(
        x, yscan, yconv, gb, mod3, w_out.astype(BF16), rw_lnx_g, rw_lnx_b, ln1_g, ln1_b,
        peer_wq.astype(BF16), peer_k1, peer_k2, bd, hd, alpha, topk, tm)

    ntok = bsz * seq
    ptm = min(PEER_TOK, seq)
    nj = dm // (2 * LANES)
    experts_t = (experts * nj).T.reshape(-1)
    zpart = _peer_u_call(experts_t, h2.reshape(ntok, dm), _pack_call(peer_u), ptm)
    act = _act_call(zpart, gates, ptm)
    out = _peer_v_call(experts_t, act, _pack_call(peer_v), x1.reshape(ntok, dm), mod3,
                       ln2_g, ln2_b, alpha, seq, ptm)
    return out.reshape(bsz, seq, dm)
```

```python
import functools
import math

import jax
import jax.numpy as jnp
from jax import lax
from jax.experimental import pallas as pl
from jax.experimental.pallas import tpu as pltpu

F32 = jnp.float32
BF16 = jnp.bfloat16
HIGHEST = lax.Precision.HIGHEST

LN_EPS = 1e-5
GN_EPS = 64e-5
PEER_TOPK = 16
DEPTH = 1
LANES = 128
SUBLANES = 8
ROW_TILE = 256
FRONT_TILE = 256
ROUTE_HEADS_PER_TRIP = 4
SCAN_CHUNK = 64
SCAN_ROWS = 4
PEER_TOK = 128
U_TOKENS_PER_TRIP = 8
V_TOKENS_PER_TRIP = 2
PACK_ROWS = 512
VMEM_LIMIT = 56 * 1024 * 1024


def _cparams(sem):
    return pltpu.CompilerParams(dimension_semantics=sem, vmem_limit_bytes=VMEM_LIMIT)


def _dot(a, b, precision=None):
    return jnp.dot(a, b, preferred_element_type=F32, precision=precision)


def _sigmoid(x):
    return 1.0 / (1.0 + jnp.exp(-x))


def _ln(x, eps):
    mu = jnp.mean(x, axis=-1, keepdims=True)
    xc = x - mu
    var = jnp.mean(xc * xc, axis=-1, keepdims=True)
    return xc * lax.rsqrt(var + eps)


def _head_sums(x, bd):
    hi = x.astype(BF16)
    lo = (x - hi.astype(F32)).astype(BF16)
    return _dot(hi, bd) + _dot(lo, bd)


def _const_spec(shape):
    nd = len(shape)
    return pl.BlockSpec(shape, lambda *_: (0,) * nd, pipeline_mode=pl.Buffered(1))


def _mod_kernel(c_ref, w_ref, b_ref, o_ref):
    c = c_ref[...]
    o_ref[...] = _dot(c * _sigmoid(c), w_ref[...], HIGHEST) + b_ref[...]


def _mod_call(c, cond_w, cond_b):
    bsz, dm = c.shape
    nblk = cond_w.shape[1] // dm
    return pl.pallas_call(
        _mod_kernel,
        name="mod",
        out_shape=jax.ShapeDtypeStruct((bsz, nblk * dm), F32),
        grid=(nblk,),
        in_specs=[pl.BlockSpec((bsz, dm), lambda j: (0, 0)),
                  pl.BlockSpec((dm, dm), lambda j: (0, j)),
                  pl.BlockSpec((1, dm), lambda j: (0, j))],
        out_specs=pl.BlockSpec((bsz, dm), lambda j: (0, j)),
        compiler_params=_cparams(("arbitrary",)),
    )(c, cond_w, cond_b.reshape(1, -1))


def _front_kernel(x_ref, mod_ref, win_ref, mu_ref, convw_ref, convb_ref, cg_ref, cb_ref,
                  w0_ref, a0_ref, wa2_ref, g2_ref, kkw_ref, kaw_ref, rk_ref, bd_ref,
                  rw_ref, yconv_ref, gb_ref, ubuf_ref, prev_ref, *, dm, cw, rw, ksize, halo):
    tm = x_ref.shape[1]
    first = pl.program_id(1) == 0

    @pl.when(first)
    def _():
        ubuf_ref[0:halo, :] = jnp.zeros((halo, cw), F32)
        prev_ref[...] = jnp.zeros(prev_ref.shape, F32)

    x = x_ref[0]
    mod = mod_ref[0]
    shift1 = mod[:, 0:dm]
    scale1 = mod[:, dm:2 * dm]
    h = _ln(x, LN_EPS) * (1.0 + scale1) + shift1
    p = _dot(h.astype(BF16), win_ref[...])

    u = p[:, 0:cw] * _sigmoid(p[:, cw:2 * cw])
    ubuf_ref[halo:halo + tm, :] = u
    acc = jnp.zeros((tm, cw), F32) + convb_ref[...]
    off = halo - (ksize - 1)
    ub = ubuf_ref[...]
    nrow = tm + halo
    for b in range(SUBLANES):
        xb = ub if b == 0 else pltpu.roll(ub, nrow - b, 0)
        for j in range(ksize):
            if (off + j) % SUBLANES == b:
                a8 = off + j - b
                acc = acc + convw_ref[j:j + 1, :] * xb[a8:a8 + tm, :]
    ubuf_ref[0:halo, :] = ubuf_ref[tm:tm + halo, :]
    yc = _ln(acc, LN_EPS) * cg_ref[...] + cb_ref[...]
    yconv_ref[0] = yc * _sigmoid(yc)

    prw = p[:, 2 * cw:]
    rolled = pltpu.roll(prw, 1, 0)
    row = lax.broadcasted_iota(jnp.int32, prw.shape, 0)
    p_prev = jnp.where(row == 0, prev_ref[0:1, :], rolled)
    prev_ref[0:1, :] = prw[tm - 1:tm, :]
    xm = prw + (p_prev - prw) * mu_ref[...]
    r = xm[:, 0:rw]
    k = xm[:, rw:2 * rw]
    v = xm[:, 2 * rw:3 * rw]
    wa = xm[:, 3 * rw:3 * rw + LANES]
    gd = xm[:, 3 * rw + LANES:3 * rw + 2 * LANES]
    lane = lax.broadcasted_iota(jnp.int32, wa.shape, 1)
    wa = jnp.where(lane < LANES // 2, jnp.tanh(wa), wa)
    t12 = _dot(wa, wa2_ref[...], HIGHEST)
    g = _dot(_sigmoid(gd), g2_ref[...], HIGHEST)
    y = w0_ref[...] + t12[:, 0:rw]
    sp = jnp.maximum(-y, 0.0) + jnp.log(1.0 + jnp.exp(-jnp.abs(y)))
    lw = -jnp.exp(-sp - 0.5)
    a = _sigmoid(a0_ref[...] + t12[:, rw:2 * rw])
    kk = k * kkw_ref[...]
    ss = _head_sums(kk * kk, bd_ref[...])
    kk = kk / jnp.maximum(jnp.sqrt(ss), 1e-12)
    kmod = k * (1.0 + (a - 1.0) * kaw_ref[...])
    bonus = _head_sums(r * kmod * rk_ref[...], bd_ref[...]) * v
    rw_ref[0, :, 0:rw] = r
    rw_ref[0, :, rw:2 * rw] = kmod
    rw_ref[0, :, 2 * rw:3 * rw] = v
    rw_ref[0, :, 3 * rw:4 * rw] = kk
    rw_ref[0, :, 4 * rw:5 * rw] = kk * a
    rw_ref[0, :, 5 * rw:6 * rw] = lw
    gb_ref[0, :, 0:rw] = g
    gb_ref[0, :, rw:2 * rw] = bonus


def _front_call(x, mod3, w_in, mu_shift, conv_w, conv_b, cg, cb, w0, a0, wa2, g2, kkw, kaw,
                rk, bd, tm):
    bsz, seq, dm = x.shape
    ksize, cw = conv_w.shape
    rw = w0.shape[-1]
    halo = 32
    assert ksize - 1 <= halo <= tm
    row = lambda a: a.reshape(1, -1)
    consts = [w_in, row(mu_shift), conv_w, row(conv_b), row(cg), row(cb), row(w0), row(a0),
              wa2, g2, row(kkw), row(kaw), row(rk), bd]
    kern = functools.partial(_front_kernel, dm=dm, cw=cw, rw=rw, ksize=ksize, halo=halo)
    return pl.pallas_call(
        kern,
        name="front",
        out_shape=(jax.ShapeDtypeStruct((bsz, seq, 6 * rw), F32),
                   jax.ShapeDtypeStruct((bsz, seq, cw), F32),
                   jax.ShapeDtypeStruct((bsz, seq, 2 * rw), F32)),
        grid=(bsz, seq // tm),
        in_specs=[pl.BlockSpec((1, tm, dm), lambda b, s: (b, s, 0)),
                  pl.BlockSpec((1, 1, mod3.shape[-1]), lambda b, s: (b, 0, 0))]
                 + [_const_spec(a.shape) for a in consts],
        out_specs=(pl.BlockSpec((1, tm, 6 * rw), lambda b, s: (b, s, 0)),
                   pl.BlockSpec((1, tm, cw), lambda b, s: (b, s, 0)),
                   pl.BlockSpec((1, tm, 2 * rw), lambda b, s: (b, s, 0))),
        scratch_shapes=[pltpu.VMEM((tm + halo, cw), F32),
                        pltpu.VMEM((8, w_in.shape[1] - 2 * cw), F32)],
        compiler_params=_cparams(("arbitrary", "arbitrary")),
    )(x, mod3, *consts)


SCAN_PASSES = 1
_NN = (((2,), (1,)), ((0,), (0,)))
_NT = (((2,), (2,)), ((0,), (0,)))
_TN = (((1,), (1,)), ((0,), (0,)))


def _split(x, passes=3):
    hi = x.astype(BF16)
    if passes == 1:
        return hi, None
    return hi, (x - hi.astype(F32)).astype(BF16)


def _mm3(a, b, dims):
    (ah, al), (bh, bl) = a, b
    dg = lambda p, q: lax.dot_general(p, q, dims, preferred_element_type=F32)
    out = dg(ah, bh)
    if bl is not None:
        out = out + dg(ah, bl)
    if al is not None:
        out = out + dg(al, bh)
    return out


def _scan_kernel(rw_ref, tri_ref, y_ref, state_ref, *, nh, hd):
    L = rw_ref.shape[1]
    rwid = nh * hd

    @pl.when(pl.program_id(1) == 0)
    def _():
        state_ref[...] = jnp.zeros(state_ref.shape, F32)

    nrow = rw_ref.shape[0]
    tri = tri_ref[...]
    parts = [[] for _ in range(6)]
    for b in range(nrow):
        blk = rw_ref[b]
        lw = blk[:, 5 * rwid:6 * rwid]
        c = _dot(tri, lw, HIGHEST)
        e_pos = jnp.exp(c)
        e_neg = jnp.exp(-c)
        parts[0].append(blk[:, 0:rwid] * e_pos)
        parts[1].append(blk[:, rwid:2 * rwid] * e_neg)
        parts[2].append(blk[:, 3 * rwid:4 * rwid] * jnp.exp(c - lw))
        parts[3].append(blk[:, 4 * rwid:5 * rwid] * e_neg)
        parts[4].append(blk[:, 2 * rwid:3 * rwid])
        parts[5].append(e_pos[L - 1:L, :])

    def heads(ts):
        return jnp.stack([t[:, h * hd:(h + 1) * hd] for t in ts for h in range(nh)], axis=0)

    rt, kh, kt, bh, v, plh = (heads(p) for p in parts)
    s0 = state_ref[...]

    strict = (tri - jnp.eye(L, dtype=F32))[None]
    incl = tri[None]
    sp = functools.partial(_split, passes=SCAN_PASSES)
    kt_s, bh_s, kh_s, rt_s, v_s, s0_s = (sp(t) for t in (kt, bh, kh, rt, v, s0))
    t_bb = _mm3(kt_s, bh_s, _NT) * strict
    t_bk = _mm3(kt_s, kh_s, _NT) * strict
    g_rb = _mm3(rt_s, bh_s, _NT) * incl
    g_rk = _mm3(rt_s, kh_s, _NT) * incl

    w = _mm3(kt_s, s0_s, _NT) + _mm3(sp(t_bk), v_s, _NN)
    tp_s = sp(t_bb)
    w = w - _mm3(tp_s, sp(w), _NN)
    n = 2
    while n < L:
        tp_s = sp(_mm3(tp_s, tp_s, _NN))
        w = w + _mm3(tp_s, sp(w), _NN)
        n *= 2
    u_s = sp(w)
    y = (_mm3(rt_s, s0_s, _NT) - _mm3(sp(g_rb), u_s, _NN)
         + _mm3(sp(g_rk), v_s, _NN))
    state_ref[...] = (s0 * plh + _mm3(v_s, sp(kh * plh), _TN)
                      - _mm3(u_s, sp(bh * plh), _TN))
    for b in range(nrow):
        y_ref[b] = jnp.concatenate([y[b * nh + h] for h in range(nh)], axis=-1)


def _scan_call(rwpack, nh, hd, chunk):
    bsz, seq, w6 = rwpack.shape
    rwid = nh * hd
    nrow = SCAN_ROWS if bsz % SCAN_ROWS == 0 else 1
    tri = jnp.tril(jnp.ones((chunk, chunk), F32))
    kern = functools.partial(_scan_kernel, nh=nh, hd=hd)
    return pl.pallas_call(
        kern,
        name="rwkv_scan",
        out_shape=jax.ShapeDtypeStruct((bsz, seq, rwid), F32),
        grid=(bsz // nrow, seq // chunk),
        in_specs=[pl.BlockSpec((nrow, chunk, w6), lambda b, s: (b, s, 0)),
                  _const_spec(tri.shape)],
        out_specs=pl.BlockSpec((nrow, chunk, rwid), lambda b, s: (b, s, 0)),
        scratch_shapes=[pltpu.VMEM((nrow * nh, hd, hd), F32)],
        compiler_params=_cparams(("arbitrary", "arbitrary")),
    )(rwpack, tri)


def _extract_max(s, iota, fill, remove=True):
    m = jnp.max(s, axis=0, keepdims=True)
    idx = jnp.min(jnp.where(s == m, iota, fill), axis=0, keepdims=True)
    return m, idx, (jnp.where(iota == idx, -jnp.inf, s) if remove else s)


def _topk_cols(s, iota, fill, k):
    vals, idxs = [], []
    for r in range(k):
        m, idx, s = _extract_max(s, iota, fill, remove=r + 1 < k)
        vals.append(m)
        idxs.append(idx)
    return jnp.concatenate(vals, axis=0), jnp.concatenate(idxs, axis=0)


def _topk_pair_sums(v1, v2, k):
    rows = lax.broadcasted_iota(jnp.int32, v1.shape, 0)
    base = rows * k
    ptr = jnp.zeros(v1.shape, jnp.int32)
    heads = v1 + v2[0:1, :]
    vals, picks = [], []
    for r in range(k):
        flat = base + ptr
        m, idx, _ = _extract_max(heads, flat, k * k, remove=False)
        vals.append(m)
        picks.append(idx)
        if r + 1 < k:
            hit = flat == idx
            nb = (idx & (k - 1)) + 1
            nxt = jnp.sum(jnp.where(rows == nb, v2, 0.0), axis=0, keepdims=True)
            nxt = jnp.where(nb < k, nxt, -jnp.inf)
            heads = jnp.where(hit, v1 + nxt, heads)
            ptr = jnp.where(hit, ptr + 1, ptr)
    return jnp.concatenate(vals, axis=0), jnp.concatenate(picks, axis=0)


def _take_rows(table, idx):
    out = jnp.zeros(idx.shape, table.dtype)
    for a in range(table.shape[0]):
        out = jnp.where(idx == a, table[a:a + 1, :], out)
    return out


def _post_kernel(x_ref, y_ref, yconv_ref, gb_ref, mod_ref, wout_ref, lnxg_ref, lnxb_ref,
                 ln1g_ref, ln1b_ref, wq_ref, k1_ref, k2_ref, bd_ref,
                 x1_ref, h2_ref, e_ref, gate_ref, q_ref, *, dm, rw, hd, alpha, nkeys, topk):
    tm = x_ref.shape[1]
    nheads = k1_ref.shape[0]
    half = k1_ref.shape[2]
    x = x_ref[0]
    mod = mod_ref[0]
    gate1 = mod[:, 2 * dm:3 * dm]
    shift2 = mod[:, 3 * dm:4 * dm]
    scale2 = mod[:, 4 * dm:5 * dm]

    y = y_ref[0]
    bd = bd_ref[...]
    mu = _head_sums(y, bd) * (1.0 / hd)
    yc = y - mu
    var = _head_sums(yc * yc, bd) * (1.0 / hd)
    gn = yc * lax.rsqrt(var + GN_EPS)
    gb = gb_ref[0]
    y_rw = (gn * lnxg_ref[...] + lnxb_ref[...] + gb[:, rw:2 * rw]) * gb[:, 0:rw]
    ycat = jnp.concatenate([yconv_ref[0], y_rw], axis=-1)
    y1 = _dot(ycat.astype(BF16), wout_ref[...])
    x1 = _ln(alpha * x + gate1 * y1, LN_EPS) * ln1g_ref[...] + ln1b_ref[...]
    x1_ref[0] = x1
    h2 = _ln(x1, LN_EPS) * (1.0 + scale2) + shift2
    h2_ref[0] = h2
    q = _dot(h2.astype(BF16), wq_ref[...])
    for i in range(2 * nheads):
        q_ref[i] = q[:, i * half:(i + 1) * half]

    shift = topk.bit_length() - 1
    assert topk == 1 << shift
    key_iota = lax.broadcasted_iota(jnp.int32, (nkeys, LANES), 0)

    def route(h, c0):
        s1 = lax.dot_general(k1_ref[h], q_ref[2 * h, c0:c0 + LANES, :], (((1,), (1,)), ((), ())),
                             preferred_element_type=F32, precision=HIGHEST)
        s2 = lax.dot_general(k2_ref[h], q_ref[2 * h + 1, c0:c0 + LANES, :],
                             (((1,), (1,)), ((), ())),
                             preferred_element_type=F32, precision=HIGHEST)
        v1, i1 = _topk_cols(s1, key_iota, nkeys, topk)
        v2, i2 = _topk_cols(s2, key_iota, nkeys, topk)
        sc, pick = _topk_pair_sums(v1, v2, topk)
        ex = (_take_rows(i1, lax.shift_right_logical(pick, shift)) * nkeys
              + _take_rows(i2, pick & (topk - 1)))
        pexp = jnp.exp(sc - sc[0:1, :])
        gates = pexp / jnp.sum(pexp, axis=0, keepdims=True)
        base = pl.multiple_of(h * topk, topk)
        e_ref[pl.ds(base, topk), c0:c0 + LANES] = ex
        gate_ref[pl.ds(base, topk), c0:c0 + LANES] = gates

    def head_body(hp, carry):
        for dh in range(ROUTE_HEADS_PER_TRIP):
            for c0 in range(0, tm, LANES):
                route(ROUTE_HEADS_PER_TRIP * hp + dh, c0)
        return carry

    lax.fori_loop(0, nheads // ROUTE_HEADS_PER_TRIP, head_body, 0)


def _post_call(x, yscan, yconv, gb, mod3, w_out, lnxg, lnxb, ln1g, ln1b, wq, k1, k2, bd,
               hd, alpha, topk, tm):
    bsz, seq, dm = x.shape
    rw = yscan.shape[-1]
    cw = yconv.shape[-1]
    nheads, nkeys, half = k1.shape
    nslot = nheads * topk
    ntok = bsz * seq
    nst = seq // tm
    row = lambda a: a.reshape(1, -1)
    consts = [w_out, row(lnxg), row(lnxb), row(ln1g), row(ln1b), wq, k1, k2, bd]
    kern = functools.partial(_post_kernel, dm=dm, rw=rw, hd=hd, alpha=alpha, nkeys=nkeys,
                             topk=topk)
    tok = lambda w: pl.BlockSpec((1, tm, w), lambda b, s: (b, s, 0))
    slot = pl.BlockSpec((nslot, tm), lambda b, s: (0, b * nst + s))
    return pl.pallas_call(
        kern,
        name="post_route",
        out_shape=(jax.ShapeDtypeStruct((bsz, seq, dm), F32),
                   jax.ShapeDtypeStruct((bsz, seq, dm), F32),
                   jax.ShapeDtypeStruct((nslot, ntok), jnp.int32),
                   jax.ShapeDtypeStruct((nslot, ntok), F32)),
        grid=(bsz, nst),
        in_specs=[tok(dm), tok(rw), tok(cw), tok(2 * rw),
                  pl.BlockSpec((1, 1, mod3.shape[-1]), lambda b, s: (b, 0, 0))]
                 + [_const_spec(a.shape) for a in consts],
        out_specs=(tok(dm), tok(dm), slot, slot),
        scratch_shapes=[pltpu.VMEM((2 * nheads, tm, half), F32)],
        compiler_params=_cparams(("arbitrary", "arbitrary")),
    )(x, yscan, yconv, gb, mod3, *consts)


def _pack_kernel(t_ref, o_ref):
    t = t_ref[...]
    hw = t.shape[1] // 2
    hi = lax.bitcast_convert_type(t[:, 0:hw].astype(BF16).astype(F32), jnp.uint32)
    lo = lax.bitcast_convert_type(t[:, hw:].astype(BF16).astype(F32), jnp.uint32)
    o_ref[...] = (hi & jnp.uint32(0xFFFF0000)) | (lo >> 16)


def _pack_call(table):
    ne, dm = table.shape
    packed = pl.pallas_call(
        _pack_kernel,
        name="peer_pack",
        out_shape=jax.ShapeDtypeStruct((ne, dm // 2), jnp.uint32),
        grid=(ne // PACK_ROWS,),
        in_specs=[pl.BlockSpec((PACK_ROWS, dm), lambda i: (i, 0))],
        out_specs=pl.BlockSpec((PACK_ROWS, dm // 2), lambda i: (i, 0)),
        compiler_params=_cparams(("arbitrary",)),
    )(table)
    return packed.reshape(ne * (dm // (2 * LANES)), LANES)


def _unpack(w):
    hi = lax.bitcast_convert_type(w & jnp.uint32(0xFFFF0000), F32)
    lo = lax.bitcast_convert_type(w << 16, F32)
    return hi, lo


def _gather_pair(tbl_ref, e_ref, off, nj):
    rows = [tbl_ref[pl.ds(pl.multiple_of(e_ref[off + d], nj), nj), :] for d in range(2)]
    return jnp.concatenate(rows, axis=0)


def _stack_chunks(row, first, nj):
    sub = lax.broadcasted_iota(jnp.int32, (2 * nj, LANES), 0) % nj
    chunk = lambda c: jnp.broadcast_to(row[:, c * LANES:(c + 1) * LANES], (2 * nj, LANES))
    out = chunk(first)
    for q in range(1, nj):
        out = jnp.where(sub == q, chunk(first + q), out)
    return out


def _peer_u_kernel(e_ref, h_ref, tbl_ref, z_ref, p0_ref, p1_ref, *, nslot, nj):
    tm = h_ref.shape[0]
    rows = 2 * nj
    lane = lax.broadcasted_iota(jnp.int32, (rows, LANES), 1)
    z_ref[...] = jnp.zeros(z_ref.shape, F32)

    @pl.when(pl.program_id(0) == 0)
    def _():
        p1_ref[...] = jnp.zeros(p1_ref.shape, F32)

    def reduce_rows(p_ref, row, tcol):
        rs = jnp.sum(p_ref[row:row + rows, :], axis=1, keepdims=True)
        z_ref[row:row + rows, :] = jnp.where(lane == tcol, rs, z_ref[row:row + rows, :])

    def one_token(t, p_ref, q_ref):
        hrow = h_ref[pl.ds(t, 1), :]
        h_hi = _stack_chunks(hrow, 0, nj)
        h_lo = _stack_chunks(hrow, nj, nj)
        ev = e_ref.at[pl.ds(t * nslot, nslot)]
        for i in range(0, nslot, 2):
            hi, lo = _unpack(_gather_pair(tbl_ref, ev, i, nj))
            p_ref[i * nj:i * nj + rows, :] = hi * h_hi + lo * h_lo
            reduce_rows(q_ref, i * nj, t - 1)

    def tok_body(i, carry):
        for d in range(0, U_TOKENS_PER_TRIP, 2):
            one_token(U_TOKENS_PER_TRIP * i + d, p0_ref, p1_ref)
            one_token(U_TOKENS_PER_TRIP * i + d + 1, p1_ref, p0_ref)
        return carry

    lax.fori_loop(0, tm // U_TOKENS_PER_TRIP, tok_body, 0)
    for r in range(0, nslot * nj, rows):
        reduce_rows(p1_ref, r, tm - 1)


def _peer_u_call(experts, h2, tbl, tm):
    ntok, dm = h2.shape
    nslot = experts.shape[0] // ntok
    nj = dm // (2 * LANES)
    kern = functools.partial(_peer_u_kernel, nslot=nslot, nj=nj)
    return pl.pallas_call(
        kern,
        name="peer_u",
        out_shape=jax.ShapeDtypeStruct((nslot * nj, ntok), F32),
        grid=(ntok // tm,),
        in_specs=[pl.BlockSpec((tm * nslot,), lambda i: (i,), memory_space=pltpu.SMEM),
                  pl.BlockSpec((tm, dm), lambda i: (i, 0)),
                  _const_spec(tbl.shape)],
        out_specs=pl.BlockSpec((nslot * nj, tm), lambda i: (0, i)),
        scratch_shapes=[pltpu.VMEM((nslot * nj, LANES), F32),
                        pltpu.VMEM((nslot * nj, LANES), F32)],
        compiler_params=_cparams(("arbitrary",)),
    )(experts, h2, tbl)


def _act_kernel(z_ref, sel_ref, gate_ref, a_ref):
    zp = z_ref[...]
    zh = zp.astype(BF16)
    zl = (zp - zh.astype(F32)).astype(BF16)
    sel = sel_ref[...]
    z = _dot(sel, zh) + _dot(sel, zl)
    a = 0.5 * z * (1.0 + lax.erf(z * (1.0 / math.sqrt(2.0)))) * gate_ref[...]
    a_ref[...] = a.T


def _act_call(zpart, gates, tm):
    nslot, ntok = gates.shape
    nj = zpart.shape[0] // nslot
    sel = (jnp.arange(nslot * nj)[None, :] // nj == jnp.arange(nslot)[:, None]).astype(BF16)
    return pl.pallas_call(
        _act_kernel,
        name="peer_act",
        out_shape=jax.ShapeDtypeStruct((ntok, nslot), F32),
        grid=(ntok // tm,),
        in_specs=[pl.BlockSpec((nslot * nj, tm), lambda i: (0, i)),
                  _const_spec(sel.shape),
                  pl.BlockSpec((nslot, tm), lambda i: (0, i))],
        out_specs=pl.BlockSpec((tm, nslot), lambda i: (i, 0)),
        compiler_params=_cparams(("arbitrary",)),
    )(zpart, sel, gates)


def _peer_v_kernel(e_ref, act_ref, tbl_ref, x_ref, mod_ref, g_ref, b_ref, o_ref,
                   y_ref, w0_ref, w1_ref, *, nslot, nj, nacc, alpha):
    tm, dm = o_ref.shape
    rows = 2 * nj
    upper = lax.broadcasted_iota(jnp.int32, (rows, LANES), 0) >= nj

    def expand(t, w_ref):
        w_ref[...] = jnp.broadcast_to(act_ref[pl.ds(t, 1), :], (nslot, nslot)).T

    def one_token(t, w_ref, wn_ref):
        expand(jnp.minimum(t + 1, tm - 1), wn_ref)
        ev = e_ref.at[pl.ds(t * nslot, nslot)]
        accs = [jnp.zeros((rows, LANES), F32) for _ in range(2 * nacc)]
        for i in range(0, nslot, 2):
            hi, lo = _unpack(_gather_pair(tbl_ref, ev, i, nj))
            a = jnp.where(upper, w_ref[i + 1:i + 2, :], w_ref[i:i + 1, :])
            j = (i // 2) % nacc
            accs[2 * j] = accs[2 * j] + a * hi
            accs[2 * j + 1] = accs[2 * j + 1] + a * lo
        acc_hi = sum(accs[2::2], accs[0])
        acc_lo = sum(accs[3::2], accs[1])
        y_ref[t] = jnp.concatenate([acc_hi[0:nj] + acc_hi[nj:], acc_lo[0:nj] + acc_lo[nj:]],
                                   axis=0)

    expand(0, w0_ref)

    def tok_body(i, carry):
        for d in range(0, V_TOKENS_PER_TRIP, 2):
            one_token(V_TOKENS_PER_TRIP * i + d, w0_ref, w1_ref)
            one_token(V_TOKENS_PER_TRIP * i + d + 1, w1_ref, w0_ref)
        return carry

    lax.fori_loop(0, tm // V_TOKENS_PER_TRIP, tok_body, 0)

    gate2 = mod_ref[0][:, 5 * dm:6 * dm]
    for r in range(0, tm, SUBLANES):
        y2 = jnp.concatenate([y_ref[r:r + SUBLANES, c, :] for c in range(2 * nj)], axis=1)
        z = alpha * x_ref[r:r + SUBLANES, :] + gate2 * y2
        o_ref[r:r + SUBLANES, :] = _ln(z, LN_EPS) * g_ref[...] + b_ref[...]


def _peer_v_call(experts, act, tbl, x1, mod3, g, b, alpha, seq, tm):
    ntok, nslot = act.shape
    dm = x1.shape[-1]
    nj = dm // (2 * LANES)
    assert nslot == LANES and seq % tm == 0
    kern = functools.partial(_peer_v_kernel, nslot=nslot, nj=nj, nacc=2, alpha=alpha)
    tok = pl.BlockSpec((tm, dm), lambda i: (i, 0))
    return pl.pallas_call(
        kern,
        name="peer_v",
        out_shape=jax.ShapeDtypeStruct((ntok, dm), F32),
        grid=(ntok // tm,),
        in_specs=[pl.BlockSpec((tm * nslot,), lambda i: (i,), memory_space=pltpu.SMEM),
                  pl.BlockSpec((tm, nslot), lambda i: (i, 0)),
                  _const_spec(tbl.shape),
                  tok,
                  pl.BlockSpec((1, 1, mod3.shape[-1]), lambda i: (i * tm // seq, 0, 0)),
                  _const_spec((1, dm)), _const_spec((1, dm))],
        out_specs=tok,
        scratch_shapes=[pltpu.VMEM((tm, 2 * nj, LANES), F32),
                        pltpu.VMEM((nslot, LANES), F32), pltpu.VMEM((nslot, LANES), F32)],
        compiler_params=_cparams(("arbitrary",)),
    )(experts, act, tbl, x1, mod3, g.reshape(1, -1), b.reshape(1, -1))


def kernel(x, c, cond_w, cond_b, w_in, mu_shift, conv_w, conv_b, conv_ln_g, conv_ln_b, rw_w0, rw_w2, rw_a0, rw_a2, rw_g2, rw_kk, rw_ka, rw_rk, rw_lnx_g, rw_lnx_b, w_out, ln1_g, ln1_b, peer_wq, peer_k1, peer_k2, peer_u, peer_v, ln2_g, ln2_b):
    bsz, seq, dm = x.shape
    nh, hd = rw_rk.shape
    rw = nh * hd
    lora_w, lora_a = rw_w2.shape[0], rw_a2.shape[0]
    assert lora_w == lora_a == LANES // 2 and rw_g2.shape[0] == LANES
    topk = PEER_TOPK
    alpha = (2.0 * DEPTH) ** 0.25
    tm = min(ROW_TILE, seq)
    chunk = min(SCAN_CHUNK, seq)

    mod3 = _mod_call(c, cond_w, cond_b).reshape(bsz, 1, -1)

    head_id = jnp.arange(rw) // hd
    bd = (head_id[:, None] == head_id[None, :]).astype(BF16)
    wa2 = jnp.zeros((LANES, 2 * rw), F32)
    wa2 = wa2.at[0:lora_w, 0:rw].set(rw_w2).at[lora_w:, rw:].set(rw_a2)

    rwpack, yconv, gb = _front_call(
        x, mod3, w_in.astype(BF16), mu_shift, conv_w, conv_b, conv_ln_g, conv_ln_b, rw_w0,
        rw_a0, wa2, rw_g2, rw_kk, rw_ka, rw_rk.reshape(-1), bd, min(FRONT_TILE, seq))
    yscan = _scan_call(rwpack, nh, hd, chunk)
    x1, h2, experts, gates = _post_call(
        x, yscan, yconv, gb, mod3, w_out.astype(BF16), rw_lnx_g, rw_lnx_b, ln1_g, ln1_b,
        peer_wq.astype(BF16), peer_k1, peer_k2, bd, hd, alpha, topk, tm)

    ntok = bsz * seq
    ptm = min(PEER_TOK, seq)
    nj = dm // (2 * LANES)
    experts_t = (experts * nj).T.reshape(-1)
    zpart = _peer_u_call(experts_t, h2.reshape(ntok, dm), _pack_call(peer_u), ptm)
    act = _act_call(zpart, gates, ptm)
    out = _peer_v_call(experts_t, act, _pack_call(peer_v), x1.reshape(ntok, dm), mod3,
                       ln2_g, ln2_b, alpha, seq, ptm)
    return out.reshape(bsz, seq, dm)
```

```python
import functools
import math

import jax
import jax.numpy as jnp
from jax import lax
from jax.experimental import pallas as pl
from jax.experimental.pallas import tpu as pltpu

F32 = jnp.float32
BF16 = jnp.bfloat16
HIGHEST = lax.Precision.HIGHEST

LN_EPS = 1e-5
GN_EPS = 64e-5
PEER_TOPK = 16
DEPTH = 1
LANES = 128
SUBLANES = 8
ROW_TILE = 256
FRONT_TILE = 256
ROUTE_HEADS_PER_TRIP = 4
SCAN_CHUNK = 64
SCAN_ROWS = 4
PEER_TOK = 128
U_TOKENS_PER_TRIP = 32
V_TOKENS_PER_TRIP = 2
PACK_ROWS = 512
VMEM_LIMIT = 56 * 1024 * 1024


def _cparams(sem):
    return pltpu.CompilerParams(dimension_semantics=sem, vmem_limit_bytes=VMEM_LIMIT)


def _dot(a, b, precision=None):
    return jnp.dot(a, b, preferred_element_type=F32, precision=precision)


def _sigmoid(x):
    return 1.0 / (1.0 + jnp.exp(-x))


def _ln(x, eps):
    mu = jnp.mean(x, axis=-1, keepdims=True)
    xc = x - mu
    var = jnp.mean(xc * xc, axis=-1, keepdims=True)
    return xc * lax.rsqrt(var + eps)


def _head_sums(x, bd):
    hi = x.astype(BF16)
    lo = (x - hi.astype(F32)).astype(BF16)
    return _dot(hi, bd) + _dot(lo, bd)


def _const_spec(shape):
    nd = len(shape)
    return pl.BlockSpec(shape, lambda *_: (0,) * nd, pipeline_mode=pl.Buffered(1))


def _mod_kernel(c_ref, w_ref, b_ref, o_ref):
    c = c_ref[...]
    o_ref[...] = _dot(c * _sigmoid(c), w_ref[...], HIGHEST) + b_ref[...]


def _mod_call(c, cond_w, cond_b):
    bsz, dm = c.shape
    nblk = cond_w.shape[1] // dm
    return pl.pallas_call(
        _mod_kernel,
        name="mod",
        out_shape=jax.ShapeDtypeStruct((bsz, nblk * dm), F32),
        grid=(nblk,),
        in_specs=[pl.BlockSpec((bsz, dm), lambda j: (0, 0)),
                  pl.BlockSpec((dm, dm), lambda j: (0, j)),
                  pl.BlockSpec((1, dm), lambda j: (0, j))],
        out_specs=pl.BlockSpec((bsz, dm), lambda j: (0, j)),
        compiler_params=_cparams(("arbitrary",)),
    )(c, cond_w, cond_b.reshape(1, -1))


def _front_kernel(x_ref, mod_ref, win_ref, mu_ref, convw_ref, convb_ref, cg_ref, cb_ref,
                  w0_ref, a0_ref, wa2_ref, g2_ref, kkw_ref, kaw_ref, rk_ref, bd_ref,
                  rw_ref, yconv_ref, gb_ref, ubuf_ref, prev_ref, *, dm, cw, rw, ksize, halo):
    tm = x_ref.shape[1]
    first = pl.program_id(1) == 0

    @pl.when(first)
    def _():
        ubuf_ref[0:halo, :] = jnp.zeros((halo, cw), F32)
        prev_ref[...] = jnp.zeros(prev_ref.shape, F32)

    x = x_ref[0]
    mod = mod_ref[0]
    shift1 = mod[:, 0:dm]
    scale1 = mod[:, dm:2 * dm]
    h = _ln(x, LN_EPS) * (1.0 + scale1) + shift1
    p = _dot(h.astype(BF16), win_ref[...])

    u = p[:, 0:cw] * _sigmoid(p[:, cw:2 * cw])
    ubuf_ref[halo:halo + tm, :] = u
    acc = jnp.zeros((tm, cw), F32) + convb_ref[...]
    off = halo - (ksize - 1)
    ub = ubuf_ref[...]
    nrow = tm + halo
    for b in range(SUBLANES):
        xb = ub if b == 0 else pltpu.roll(ub, nrow - b, 0)
        for j in range(ksize):
            if (off + j) % SUBLANES == b:
                a8 = off + j - b
                acc = acc + convw_ref[j:j + 1, :] * xb[a8:a8 + tm, :]
    ubuf_ref[0:halo, :] = ubuf_ref[tm:tm + halo, :]
    yc = _ln(acc, LN_EPS) * cg_ref[...] + cb_ref[...]
    yconv_ref[0] = yc * _sigmoid(yc)

    prw = p[:, 2 * cw:]
    rolled = pltpu.roll(prw, 1, 0)
    row = lax.broadcasted_iota(jnp.int32, prw.shape, 0)
    p_prev = jnp.where(row == 0, prev_ref[0:1, :], rolled)
    prev_ref[0:1, :] = prw[tm - 1:tm, :]
    xm = prw + (p_prev - prw) * mu_ref[...]
    r = xm[:, 0:rw]
    k = xm[:, rw:2 * rw]
    v = xm[:, 2 * rw:3 * rw]
    wa = xm[:, 3 * rw:3 * rw + LANES]
    gd = xm[:, 3 * rw + LANES:3 * rw + 2 * LANES]
    lane = lax.broadcasted_iota(jnp.int32, wa.shape, 1)
    wa = jnp.where(lane < LANES // 2, jnp.tanh(wa), wa)
    t12 = _dot(wa, wa2_ref[...], HIGHEST)
    g = _dot(_sigmoid(gd), g2_ref[...], HIGHEST)
    y = w0_ref[...] + t12[:, 0:rw]
    sp = jnp.maximum(-y, 0.0) + jnp.log(1.0 + jnp.exp(-jnp.abs(y)))
    lw = -jnp.exp(-sp - 0.5)
    a = _sigmoid(a0_ref[...] + t12[:, rw:2 * rw])
    kk = k * kkw_ref[...]
    ss = _head_sums(kk * kk, bd_ref[...])
    kk = kk / jnp.maximum(jnp.sqrt(ss), 1e-12)
    kmod = k * (1.0 + (a - 1.0) * kaw_ref[...])
    bonus = _head_sums(r * kmod * rk_ref[...], bd_ref[...]) * v
    rw_ref[0, :, 0:rw] = r
    rw_ref[0, :, rw:2 * rw] = kmod
    rw_ref[0, :, 2 * rw:3 * rw] = v
    rw_ref[0, :, 3 * rw:4 * rw] = kk
    rw_ref[0, :, 4 * rw:5 * rw] = kk * a
    rw_ref[0, :, 5 * rw:6 * rw] = lw
    gb_ref[0, :, 0:rw] = g
    gb_ref[0, :, rw:2 * rw] = bonus


def _front_call(x, mod3, w_in, mu_shift, conv_w, conv_b, cg, cb, w0, a0, wa2, g2, kkw, kaw,
                rk, bd, tm):
    bsz, seq, dm = x.shape
    ksize, cw = conv_w.shape
    rw = w0.shape[-1]
    halo = 32
    assert ksize - 1 <= halo <= tm
    row = lambda a: a.reshape(1, -1)
    consts = [w_in, row(mu_shift), conv_w, row(conv_b), row(cg), row(cb), row(w0), row(a0),
              wa2, g2, row(kkw), row(kaw), row(rk), bd]
    kern = functools.partial(_front_kernel, dm=dm, cw=cw, rw=rw, ksize=ksize, halo=halo)
    return pl.pallas_call(
        kern,
        name="front",
        out_shape=(jax.ShapeDtypeStruct((bsz, seq, 6 * rw), F32),
                   jax.ShapeDtypeStruct((bsz, seq, cw), F32),
                   jax.ShapeDtypeStruct((bsz, seq, 2 * rw), F32)),
        grid=(bsz, seq // tm),
        in_specs=[pl.BlockSpec((1, tm, dm), lambda b, s: (b, s, 0)),
                  pl.BlockSpec((1, 1, mod3.shape[-1]), lambda b, s: (b, 0, 0))]
                 + [_const_spec(a.shape) for a in consts],
        out_specs=(pl.BlockSpec((1, tm, 6 * rw), lambda b, s: (b, s, 0)),
                   pl.BlockSpec((1, tm, cw), lambda b, s: (b, s, 0)),
                   pl.BlockSpec((1, tm, 2 * rw), lambda b, s: (b, s, 0))),
        scratch_shapes=[pltpu.VMEM((tm + halo, cw), F32),
                        pltpu.VMEM((8, w_in.shape[1] - 2 * cw), F32)],
        compiler_params=_cparams(("arbitrary", "arbitrary")),
    )(x, mod3, *consts)


SCAN_PASSES = 1
_NN = (((2,), (1,)), ((0,), (0,)))
_NT = (((2,), (2,)), ((0,), (0,)))
_TN = (((1,), (1,)), ((0,), (0,)))


def _split(x, passes=3):
    hi = x.astype(BF16)
    if passes == 1:
        return hi, None
    return hi, (x - hi.astype(F32)).astype(BF16)


def _mm3(a, b, dims):
    (ah, al), (bh, bl) = a, b
    dg = lambda p, q: lax.dot_general(p, q, dims, preferred_element_type=F32)
    out = dg(ah, bh)
    if bl is not None:
        out = out + dg(ah, bl)
    if al is not None:
        out = out + dg(al, bh)
    return out


def _scan_kernel(rw_ref, tri_ref, y_ref, state_ref, *, nh, hd):
    L = rw_ref.shape[1]
    rwid = nh * hd

    @pl.when(pl.program_id(1) == 0)
    def _():
        state_ref[...] = jnp.zeros(state_ref.shape, F32)

    nrow = rw_ref.shape[0]
    tri = tri_ref[...]
    parts = [[] for _ in range(6)]
    for b in range(nrow):
        blk = rw_ref[b]
        lw = blk[:, 5 * rwid:6 * rwid]
        c = _dot(tri, lw, HIGHEST)
        e_pos = jnp.exp(c)
        e_neg = jnp.exp(-c)
        parts[0].append(blk[:, 0:rwid] * e_pos)
        parts[1].append(blk[:, rwid:2 * rwid] * e_neg)
        parts[2].append(blk[:, 3 * rwid:4 * rwid] * jnp.exp(c - lw))
        parts[3].append(blk[:, 4 * rwid:5 * rwid] * e_neg)
        parts[4].append(blk[:, 2 * rwid:3 * rwid])
        parts[5].append(e_pos[L - 1:L, :])

    def heads(ts):
        return jnp.stack([t[:, h * hd:(h + 1) * hd] for t in ts for h in range(nh)], axis=0)

    rt, kh, kt, bh, v, plh = (heads(p) for p in parts)
    s0 = state_ref[...]

    strict = (tri - jnp.eye(L, dtype=F32))[None]
    incl = tri[None]
    sp = functools.partial(_split, passes=SCAN_PASSES)
    kt_s, bh_s, kh_s, rt_s, v_s, s0_s = (sp(t) for t in (kt, bh, kh, rt, v, s0))
    t_bb = _mm3(kt_s, bh_s, _NT) * strict
    t_bk = _mm3(kt_s, kh_s, _NT) * strict
    g_rb = _mm3(rt_s, bh_s, _NT) * incl
    g_rk = _mm3(rt_s, kh_s, _NT) * incl

    w = _mm3(kt_s, s0_s, _NT) + _mm3(sp(t_bk), v_s, _NN)
    tp_s = sp(t_bb)
    w = w - _mm3(tp_s, sp(w), _NN)
    n = 2
    while n < L:
        tp_s = sp(_mm3(tp_s, tp_s, _NN))
        w = w + _mm3(tp_s, sp(w), _NN)
        n *= 2
    u_s = sp(w)
    y = (_mm3(rt_s, s0_s, _NT) - _mm3(sp(g_rb), u_s, _NN)
         + _mm3(sp(g_rk), v_s, _NN))
    state_ref[...] = (s0 * plh + _mm3(v_s, sp(kh * plh), _TN)
                      - _mm3(u_s, sp(bh * plh), _TN))
    for b in range(nrow):
        y_ref[b] = jnp.concatenate([y[b * nh + h] for h in range(nh)], axis=-1)


def _scan_call(rwpack, nh, hd, chunk):
    bsz, seq, w6 = rwpack.shape
    rwid = nh * hd
    nrow = SCAN_ROWS if bsz % SCAN_ROWS == 0 else 1
    tri = jnp.tril(jnp.ones((chunk, chunk), F32))
    kern = functools.partial(_scan_kernel, nh=nh, hd=hd)
    return pl.pallas_call(
        kern,
        name="rwkv_scan",
        out_shape=jax.ShapeDtypeStruct((bsz, seq, rwid), F32),
        grid=(bsz // nrow, seq // chunk),
        in_specs=[pl.BlockSpec((nrow, chunk, w6), lambda b, s: (b, s, 0)),
                  _const_spec(tri.shape)],
        out_specs=pl.BlockSpec((nrow, chunk, rwid), lambda b, s: (b, s, 0)),
        scratch_shapes=[pltpu.VMEM((nrow * nh, hd, hd), F32)],
        compiler_params=_cparams(("arbitrary", "arbitrary")),
    )(rwpack, tri)


def _extract_max(s, iota, fill, remove=True):
    m = jnp.max(s, axis=0, keepdims=True)
    idx = jnp.min(jnp.where(s == m, iota, fill), axis=0, keepdims=True)
    return m, idx, (jnp.where(iota == idx, -jnp.inf, s) if remove else s)


def _topk_cols(s, iota, fill, k):
    vals, idxs = [], []
    for r in range(k):
        m, idx, s = _extract_max(s, iota, fill, remove=r + 1 < k)
        vals.append(m)
        idxs.append(idx)
    return jnp.concatenate(vals, axis=0), jnp.concatenate(idxs, axis=0)


def _topk_pair_sums(v1, v2, k):
    rows = lax.broadcasted_iota(jnp.int32, v1.shape, 0)
    base = rows * k
    ptr = jnp.zeros(v1.shape, jnp.int32)
    heads = v1 + v2[0:1, :]
    vals, picks = [], []
    for r in range(k):
        flat = base + ptr
        m, idx, _ = _extract_max(heads, flat, k * k, remove=False)
        vals.append(m)
        picks.append(idx)
        if r + 1 < k:
            hit = flat == idx
            nb = (idx & (k - 1)) + 1
            nxt = jnp.sum(jnp.where(rows == nb, v2, 0.0), axis=0, keepdims=True)
            nxt = jnp.where(nb < k, nxt, -jnp.inf)
            heads = jnp.where(hit, v1 + nxt, heads)
            ptr = jnp.where(hit, ptr + 1, ptr)
    return jnp.concatenate(vals, axis=0), jnp.concatenate(picks, axis=0)


def _take_rows(table, idx):
    out = jnp.zeros(idx.shape, table.dtype)
    for a in range(table.shape[0]):
        out = jnp.where(idx == a, table[a:a + 1, :], out)
    return out


def _post_kernel(x_ref, y_ref, yconv_ref, gb_ref, mod_ref, wout_ref, lnxg_ref, lnxb_ref,
                 ln1g_ref, ln1b_ref, wq_ref, k1_ref, k2_ref, bd_ref,
                 x1_ref, h2_ref, e_ref, gate_ref, q_ref, *, dm, rw, hd, alpha, nkeys, topk):
    tm = x_ref.shape[1]
    nheads = k1_ref.shape[0]
    half = k1_ref.shape[2]
    x = x_ref[0]
    mod = mod_ref[0]
    gate1 = mod[:, 2 * dm:3 * dm]
    shift2 = mod[:, 3 * dm:4 * dm]
    scale2 = mod[:, 4 * dm:5 * dm]

    y = y_ref[0]
    bd = bd_ref[...]
    mu = _head_sums(y, bd) * (1.0 / hd)
    yc = y - mu
    var = _head_sums(yc * yc, bd) * (1.0 / hd)
    gn = yc * lax.rsqrt(var + GN_EPS)
    gb = gb_ref[0]
    y_rw = (gn * lnxg_ref[...] + lnxb_ref[...] + gb[:, rw:2 * rw]) * gb[:, 0:rw]
    ycat = jnp.concatenate([yconv_ref[0], y_rw], axis=-1)
    y1 = _dot(ycat.astype(BF16), wout_ref[...])
    x1 = _ln(alpha * x + gate1 * y1, LN_EPS) * ln1g_ref[...] + ln1b_ref[...]
    x1_ref[0] = x1
    h2 = _ln(x1, LN_EPS) * (1.0 + scale2) + shift2
    h2_ref[0] = h2
    q = _dot(h2.astype(BF16), wq_ref[...])
    for i in range(2 * nheads):
        q_ref[i] = q[:, i * half:(i + 1) * half]

    shift = topk.bit_length() - 1
    assert topk == 1 << shift
    key_iota = lax.broadcasted_iota(jnp.int32, (nkeys, LANES), 0)

    def route(h, c0):
        s1 = lax.dot_general(k1_ref[h], q_ref[2 * h, c0:c0 + LANES, :], (((1,), (1,)), ((), ())),
                             preferred_element_type=F32, precision=HIGHEST)
        s2 = lax.dot_general(k2_ref[h], q_ref[2 * h + 1, c0:c0 + LANES, :],
                             (((1,), (1,)), ((), ())),
                             preferred_element_type=F32, precision=HIGHEST)
        v1, i1 = _topk_cols(s1, key_iota, nkeys, topk)
        v2, i2 = _topk_cols(s2, key_iota, nkeys, topk)
        sc, pick = _topk_pair_sums(v1, v2, topk)
        ex = (_take_rows(i1, lax.shift_right_logical(pick, shift)) * nkeys
              + _take_rows(i2, pick & (topk - 1)))
        pexp = jnp.exp(sc - sc[0:1, :])
        gates = pexp / jnp.sum(pexp, axis=0, keepdims=True)
        base = pl.multiple_of(h * topk, topk)
        e_ref[pl.ds(base, topk), c0:c0 + LANES] = ex
        gate_ref[pl.ds(base, topk), c0:c0 + LANES] = gates

    def head_body(hp, carry):
        for dh in range(ROUTE_HEADS_PER_TRIP):
            for c0 in range(0, tm, LANES):
                route(ROUTE_HEADS_PER_TRIP * hp + dh, c0)
        return carry

    lax.fori_loop(0, nheads // ROUTE_HEADS_PER_TRIP, head_body, 0)


def _post_call(x, yscan, yconv, gb, mod3, w_out, lnxg, lnxb, ln1g, ln1b, wq, k1, k2, bd,
               hd, alpha, topk, tm):
    bsz, seq, dm = x.shape
    rw = yscan.shape[-1]
    cw = yconv.shape[-1]
    nheads, nkeys, half = k1.shape
    nslot = nheads * topk
    ntok = bsz * seq
    nst = seq // tm
    row = lambda a: a.reshape(1, -1)
    consts = [w_out, row(lnxg), row(lnxb), row(ln1g), row(ln1b), wq, k1, k2, bd]
    kern = functools.partial(_post_kernel, dm=dm, rw=rw, hd=hd, alpha=alpha, nkeys=nkeys,
                             topk=topk)
    tok = lambda w: pl.BlockSpec((1, tm, w), lambda b, s: (b, s, 0))
    slot = pl.BlockSpec((nslot, tm), lambda b, s: (0, b * nst + s))
    return pl.pallas_call(
        kern,
        name="post_route",
        out_shape=(jax.ShapeDtypeStruct((bsz, seq, dm), F32),
                   jax.ShapeDtypeStruct((bsz, seq, dm), F32),
                   jax.ShapeDtypeStruct((nslot, ntok), jnp.int32),
                   jax.ShapeDtypeStruct((nslot, ntok), F32)),
        grid=(bsz, nst),
        in_specs=[tok(dm), tok(rw), tok(cw), tok(2 * rw),
                  pl.BlockSpec((1, 1, mod3.shape[-1]), lambda b, s: (b, 0, 0))]
                 + [_const_spec(a.shape) for a in consts],
        out_specs=(tok(dm), tok(dm), slot, slot),
        scratch_shapes=[pltpu.VMEM((2 * nheads, tm, half), F32)],
        compiler_params=_cparams(("arbitrary", "arbitrary")),
    )(x, yscan, yconv, gb, mod3, *consts)


def _pack_kernel(t_ref, o_ref):
    t = t_ref[...]
    hw = t.shape[1] // 2
    hi = lax.bitcast_convert_type(t[:, 0:hw].astype(BF16).astype(F32), jnp.uint32)
    lo = lax.bitcast_convert_type(t[:, hw:].astype(BF16).astype(F32), jnp.uint32)
    o_ref[...] = (hi & jnp.uint32(0xFFFF0000)) | (lo >> 16)


def _pack_call(table):
    ne, dm = table.shape
    packed = pl.pallas_call(
        _pack_kernel,
        name="peer_pack",
        out_shape=jax.ShapeDtypeStruct((ne, dm // 2), jnp.uint32),
        grid=(ne // PACK_ROWS,),
        in_specs=[pl.BlockSpec((PACK_ROWS, dm), lambda i: (i, 0))],
        out_specs=pl.BlockSpec((PACK_ROWS, dm // 2), lambda i: (i, 0)),
        compiler_params=_cparams(("arbitrary",)),
    )(table)
    return packed.reshape(ne * (dm // (2 * LANES)), LANES)


def _unpack(w):
    hi = lax.bitcast_convert_type(w & jnp.uint32(0xFFFF0000), F32)
    lo = lax.bitcast_convert_type(w << 16, F32)
    return hi, lo


def _gather_pair(tbl_ref, e_ref, off, nj):
    rows = [tbl_ref[pl.ds(pl.multiple_of(e_ref[off + d], nj), nj), :] for d in range(2)]
    return jnp.concatenate(rows, axis=0)


def _stack_chunks(row, first, nj):
    sub = lax.broadcasted_iota(jnp.int32, (2 * nj, LANES), 0) % nj
    chunk = lambda c: jnp.broadcast_to(row[:, c * LANES:(c + 1) * LANES], (2 * nj, LANES))
    out = chunk(first)
    for q in range(1, nj):
        out = jnp.where(sub == q, chunk(first + q), out)
    return out


def _peer_u_kernel(e_ref, h_ref, tbl_ref, z_ref, p0_ref, p1_ref, *, nslot, nj):
    tm = h_ref.shape[0]
    rows = 2 * nj
    lane = lax.broadcasted_iota(jnp.int32, (rows, LANES), 1)
    z_ref[...] = jnp.zeros(z_ref.shape, F32)

    @pl.when(pl.program_id(0) == 0)
    def _():
        p1_ref[...] = jnp.zeros(p1_ref.shape, F32)

    def reduce_rows(p_ref, row, tcol):
        rs = jnp.sum(p_ref[row:row + rows, :], axis=1, keepdims=True)
        z_ref[row:row + rows, :] = jnp.where(lane == tcol, rs, z_ref[row:row + rows, :])

    def one_token(t, p_ref, q_ref):
        hrow = h_ref[pl.ds(t, 1), :]
        h_hi = _stack_chunks(hrow, 0, nj)
        h_lo = _stack_chunks(hrow, nj, nj)
        ev = e_ref.at[pl.ds(t * nslot, nslot)]
        for i in range(0, nslot, 2):
            hi, lo = _unpack(_gather_pair(tbl_ref, ev, i, nj))
            p_ref[i * nj:i * nj + rows, :] = hi * h_hi + lo * h_lo
            reduce_rows(q_ref, i * nj, t - 1)

    def tok_body(i, carry):
        for d in range(0, U_TOKENS_PER_TRIP, 2):
            one_token(U_TOKENS_PER_TRIP * i + d, p0_ref, p1_ref)
            one_token(U_TOKENS_PER_TRIP * i + d + 1, p1_ref, p0_ref)
        return carry

    lax.fori_loop(0, tm // U_TOKENS_PER_TRIP, tok_body, 0)
    for r in range(0, nslot * nj, rows):
        reduce_rows(p1_ref, r, tm - 1)


def _peer_u_call(experts, h2, tbl, tm):
    ntok, dm = h2.shape
    nslot = experts.shape[0] // ntok
    nj = dm // (2 * LANES)
    kern = functools.partial(_peer_u_kernel, nslot=nslot, nj=nj)
    return pl.pallas_call(
        kern,
        name="peer_u",
        out_shape=jax.ShapeDtypeStruct((nslot * nj, ntok), F32),
        grid=(ntok // tm,),
        in_specs=[pl.BlockSpec((tm * nslot,), lambda i: (i,), memory_space=pltpu.SMEM),
                  pl.BlockSpec((tm, dm), lambda i: (i, 0)),
                  _const_spec(tbl.shape)],
        out_specs=pl.BlockSpec((nslot * nj, tm), lambda i: (0, i)),
        scratch_shapes=[pltpu.VMEM((nslot * nj, LANES), F32),
                        pltpu.VMEM((nslot * nj, LANES), F32)],
        compiler_params=_cparams(("arbitrary",)),
    )(experts, h2, tbl)


def _act_kernel(z_ref, sel_ref, gate_ref, a_ref):
    zp = z_ref[...]
    zh = zp.astype(BF16)
    zl = (zp - zh.astype(F32)).astype(BF16)
    sel = sel_ref[...]
    z = _dot(sel, zh) + _dot(sel, zl)
    a = 0.5 * z * (1.0 + lax.erf(z * (1.0 / math.sqrt(2.0)))) * gate_ref[...]
    a_ref[...] = a.T


def _act_call(zpart, gates, tm):
    nslot, ntok = gates.shape
    nj = zpart.shape[0] // nslot
    sel = (jnp.arange(nslot * nj)[None, :] // nj == jnp.arange(nslot)[:, None]).astype(BF16)
    return pl.pallas_call(
        _act_kernel,
        name="peer_act",
        out_shape=jax.ShapeDtypeStruct((ntok, nslot), F32),
        grid=(ntok // tm,),
        in_specs=[pl.BlockSpec((nslot * nj, tm), lambda i: (0, i)),
                  _const_spec(sel.shape),
                  pl.BlockSpec((nslot, tm), lambda i: (0, i))],
        out_specs=pl.BlockSpec((tm, nslot), lambda i: (i, 0)),
        compiler_params=_cparams(("arbitrary",)),
    )(zpart, sel, gates)


def _peer_v_kernel(e_ref, act_ref, tbl_ref, x_ref, mod_ref, g_ref, b_ref, o_ref,
                   y_ref, w0_ref, w1_ref, *, nslot, nj, nacc, alpha):
    tm, dm = o_ref.shape
    rows = 2 * nj
    upper = lax.broadcasted_iota(jnp.int32, (rows, LANES), 0) >= nj

    def expand(t, w_ref):
        w_ref[...] = jnp.broadcast_to(act_ref[pl.ds(t, 1), :], (nslot, nslot)).T

    def one_token(t, w_ref, wn_ref):
        expand(jnp.minimum(t + 1, tm - 1), wn_ref)
        ev = e_ref.at[pl.ds(t * nslot, nslot)]
        accs = [jnp.zeros((rows, LANES), F32) for _ in range(2 * nacc)]
        for i in range(0, nslot, 2):
            hi, lo = _unpack(_gather_pair(tbl_ref, ev, i, nj))
            a = jnp.where(upper, w_ref[i + 1:i + 2, :], w_ref[i:i + 1, :])
            j = (i // 2) % nacc
            accs[2 * j] = accs[2 * j] + a * hi
            accs[2 * j + 1] = accs[2 * j + 1] + a * lo
        acc_hi = sum(accs[2::2], accs[0])
        acc_lo = sum(accs[3::2], accs[1])
        y_ref[t] = jnp.concatenate([acc_hi[0:nj] + acc_hi[nj:], acc_lo[0:nj] + acc_lo[nj:]],
                                   axis=0)

    expand(0, w0_ref)

    def tok_body(i, carry):
        for d in range(0, V_TOKENS_PER_TRIP, 2):
            one_token(V_TOKENS_PER_TRIP * i + d, w0_ref, w1_ref)
            one_token(V_TOKENS_PER_TRIP * i + d + 1, w1_ref, w0_ref)
        return carry

    lax.fori_loop(0, tm // V_TOKENS_PER_TRIP, tok_body, 0)

    gate2 = mod_ref[0][:, 5 * dm:6 * dm]
    for r in range(0, tm, SUBLANES):
        y2 = jnp.concatenate([y_ref[r:r + SUBLANES, c, :] for c in range(2 * nj)], axis=1)
        z = alpha * x_ref[r:r + SUBLANES, :] + gate2 * y2
        o_ref[r:r + SUBLANES, :] = _ln(z, LN_EPS) * g_ref[...] + b_ref[...]


def _peer_v_call(experts, act, tbl, x1, mod3, g, b, alpha, seq, tm):
    ntok, nslot = act.shape
    dm = x1.shape[-1]
    nj = dm // (2 * LANES)
    assert nslot == LANES and seq % tm == 0
    kern = functools.partial(_peer_v_kernel, nslot=nslot, nj=nj, nacc=2, alpha=alpha)
    tok = pl.BlockSpec((tm, dm), lambda i: (i, 0))
    return pl.pallas_call(
        kern,
        name="peer_v",
        out_shape=jax.ShapeDtypeStruct((ntok, dm), F32),
        grid=(ntok // tm,),
        in_specs=[pl.BlockSpec((tm * nslot,), lambda i: (i,), memory_space=pltpu.SMEM),
                  pl.BlockSpec((tm, nslot), lambda i: (i, 0)),
                  _const_spec(tbl.shape),
                  tok,
                  pl.BlockSpec((1, 1, mod3.shape[-1]), lambda i: (i * tm // seq, 0, 0)),
                  _const_spec((1, dm)), _const_spec((1, dm))],
        out_specs=tok,
        scratch_shapes=[pltpu.VMEM((tm, 2 * nj, LANES), F32),
                        pltpu.VMEM((nslot, LANES), F32), pltpu.VMEM((nslot, LANES), F32)],
        compiler_params=_cparams(("arbitrary",)),
    )(experts, act, tbl, x1, mod3, g.reshape(1, -1), b.reshape(1, -1))


def kernel(x, c, cond_w, cond_b, w_in, mu_shift, conv_w, conv_b, conv_ln_g, conv_ln_b, rw_w0, rw_w2, rw_a0, rw_a2, rw_g2, rw_kk, rw_ka, rw_rk, rw_lnx_g, rw_lnx_b, w_out, ln1_g, ln1_b, peer_wq, peer_k1, peer_k2, peer_u, peer_v, ln2_g, ln2_b):
    bsz, seq, dm = x.shape
    nh, hd = rw_rk.shape
    rw = nh * hd
    lora_w, lora_a = rw_w2.shape[0], rw_a2.shape[0]
    assert lora_w == lora_a == LANES // 2 and rw_g2.shape[0] == LANES
    topk = PEER_TOPK
    alpha = (2.0 * DEPTH) ** 0.25
    tm = min(ROW_TILE, seq)
    chunk = min(SCAN_CHUNK, seq)

    mod3 = _mod_call(c, cond_w, cond_b).reshape(bsz, 1, -1)

    head_id = jnp.arange(rw) // hd
    bd = (head_id[:, None] == head_id[None, :]).astype(BF16)
    wa2 = jnp.zeros((LANES, 2 * rw), F32)
    wa2 = wa2.at[0:lora_w, 0:rw].set(rw_w2).at[lora_w:, rw:].set(rw_a2)

    rwpack, yconv, gb = _front_call(
        x, mod3, w_in.astype(BF16), mu_shift, conv_w, conv_b, conv_ln_g, conv_ln_b, rw_w0,
        rw_a0, wa2, rw_g2, rw_kk, rw_ka, rw_rk.reshape(-1), bd, min(FRONT_TILE, seq))
    yscan = _scan_call(rwpack, nh, hd, chunk)
    x1, h2, experts, gates = _post_call(
        x, yscan, yconv, gb, mod3, w_out.astype(BF16), rw_lnx_g, rw_lnx_b, ln1_g, ln1_b,
        peer_wq.astype(BF16), peer_k1, peer_k2, bd, hd, alpha, topk, tm)

    ntok = bsz * seq
    ptm = min(PEER_TOK, seq)
    nj = dm // (2 * LANES)
    experts_t = (experts * nj).T.reshape(-1)
    zpart = _peer_u_call(experts_t, h2.reshape(ntok, dm), _pack_call(peer_u), ptm)
    act = _act_call(zpart, gates, ptm)
    out = _peer_v_call(experts_t, act, _pack_call(peer_v), x1.reshape(ntok, dm), mod3,
                       ln2_g, ln2_b, alpha, seq, ptm)
    return out.reshape(bsz, seq, dm)
```

```python
import functools
import math

import jax
import jax.numpy as jnp
from jax import lax
from jax.experimental import pallas as pl
from jax.experimental.pallas import tpu as pltpu

F32 = jnp.float32
BF16 = jnp.bfloat16
HIGHEST = lax.Precision.HIGHEST

LN_EPS = 1e-5
GN_EPS = 64e-5
PEER_TOPK = 16
DEPTH = 1
LANES = 128
SUBLANES = 8
ROW_TILE = 256
FRONT_TILE = 256
ROUTE_HEADS_PER_TRIP = 4
SCAN_CHUNK = 64
SCAN_ROWS = 4
PEER_TOK = 128
U_TOKENS_PER_TRIP = 32
V_TOKENS_PER_TRIP = 32
TILE_PITCH = 136
BF16_ROWS = 16
PACK_ROWS = 512
VMEM_LIMIT = 56 * 1024 * 1024


def _cparams(sem):
    return pltpu.CompilerParams(dimension_semantics=sem, vmem_limit_bytes=VMEM_LIMIT)


def _dot(a, b, precision=None):
    return jnp.dot(a, b, preferred_element_type=F32, precision=precision)


def _sigmoid(x):
    return 1.0 / (1.0 + jnp.exp(-x))


def _ln(x, eps):
    mu = jnp.mean(x, axis=-1, keepdims=True)
    xc = x - mu
    var = jnp.mean(xc * xc, axis=-1, keepdims=True)
    return xc * lax.rsqrt(var + eps)


def _head_sums(x, bd):
    hi = x.astype(BF16)
    lo = (x - hi.astype(F32)).astype(BF16)
    return _dot(hi, bd) + _dot(lo, bd)


def _const_spec(shape):
    nd = len(shape)
    return pl.BlockSpec(shape, lambda *_: (0,) * nd, pipeline_mode=pl.Buffered(1))


def _mod_kernel(c_ref, w_ref, b_ref, o_ref):
    c = c_ref[...]
    o_ref[...] = _dot(c * _sigmoid(c), w_ref[...], HIGHEST) + b_ref[...]


def _mod_call(c, cond_w, cond_b):
    bsz, dm = c.shape
    nblk = cond_w.shape[1] // dm
    return pl.pallas_call(
        _mod_kernel,
        name="mod",
        out_shape=jax.ShapeDtypeStruct((bsz, nblk * dm), F32),
        grid=(nblk,),
        in_specs=[pl.BlockSpec((bsz, dm), lambda j: (0, 0)),
                  pl.BlockSpec((dm, dm), lambda j: (0, j)),
                  pl.BlockSpec((1, dm), lambda j: (0, j))],
        out_specs=pl.BlockSpec((bsz, dm), lambda j: (0, j)),
        compiler_params=_cparams(("arbitrary",)),
    )(c, cond_w, cond_b.reshape(1, -1))


def _front_kernel(x_ref, mod_ref, win_ref, mu_ref, convw_ref, convb_ref, cg_ref, cb_ref,
                  w0_ref, a0_ref, wa2_ref, g2_ref, kkw_ref, kaw_ref, rk_ref, bd_ref,
                  rw_ref, yconv_ref, gb_ref, ubuf_ref, prev_ref, *, dm, cw, rw, ksize, halo):
    tm = x_ref.shape[1]
    first = pl.program_id(1) == 0

    @pl.when(first)
    def _():
        ubuf_ref[0:halo, :] = jnp.zeros((halo, cw), F32)
        prev_ref[...] = jnp.zeros(prev_ref.shape, F32)

    x = x_ref[0]
    mod = mod_ref[0]
    shift1 = mod[:, 0:dm]
    scale1 = mod[:, dm:2 * dm]
    h = _ln(x, LN_EPS) * (1.0 + scale1) + shift1
    p = _dot(h.astype(BF16), win_ref[...])

    u = p[:, 0:cw] * _sigmoid(p[:, cw:2 * cw])
    ubuf_ref[halo:halo + tm, :] = u
    acc = jnp.zeros((tm, cw), F32) + convb_ref[...]
    off = halo - (ksize - 1)
    ub = ubuf_ref[...]
    nrow = tm + halo
    for b in range(SUBLANES):
        xb = ub if b == 0 else pltpu.roll(ub, nrow - b, 0)
        for j in range(ksize):
            if (off + j) % SUBLANES == b:
                a8 = off + j - b
                acc = acc + convw_ref[j:j + 1, :] * xb[a8:a8 + tm, :]
    ubuf_ref[0:halo, :] = ubuf_ref[tm:tm + halo, :]
    yc = _ln(acc, LN_EPS) * cg_ref[...] + cb_ref[...]
    yconv_ref[0] = yc * _sigmoid(yc)

    prw = p[:, 2 * cw:]
    rolled = pltpu.roll(prw, 1, 0)
    row = lax.broadcasted_iota(jnp.int32, prw.shape, 0)
    p_prev = jnp.where(row == 0, prev_ref[0:1, :], rolled)
    prev_ref[0:1, :] = prw[tm - 1:tm, :]
    xm = prw + (p_prev - prw) * mu_ref[...]
    r = xm[:, 0:rw]
    k = xm[:, rw:2 * rw]
    v = xm[:, 2 * rw:3 * rw]
    wa = xm[:, 3 * rw:3 * rw + LANES]
    gd = xm[:, 3 * rw + LANES:3 * rw + 2 * LANES]
    lane = lax.broadcasted_iota(jnp.int32, wa.shape, 1)
    wa = jnp.where(lane < LANES // 2, jnp.tanh(wa), wa)
    t12 = _dot(wa, wa2_ref[...], HIGHEST)
    g = _dot(_sigmoid(gd), g2_ref[...], HIGHEST)
    y = w0_ref[...] + t12[:, 0:rw]
    sp = jnp.maximum(-y, 0.0) + jnp.log(1.0 + jnp.exp(-jnp.abs(y)))
    lw = -jnp.exp(-sp - 0.5)
    a = _sigmoid(a0_ref[...] + t12[:, rw:2 * rw])
    kk = k * kkw_ref[...]
    ss = _head_sums(kk * kk, bd_ref[...])
    kk = kk / jnp.maximum(jnp.sqrt(ss), 1e-12)
    kmod = k * (1.0 + (a - 1.0) * kaw_ref[...])
    bonus = _head_sums(r * kmod * rk_ref[...], bd_ref[...]) * v
    rw_ref[0, :, 0:rw] = r
    rw_ref[0, :, rw:2 * rw] = kmod
    rw_ref[0, :, 2 * rw:3 * rw] = v
    rw_ref[0, :, 3 * rw:4 * rw] = kk
    rw_ref[0, :, 4 * rw:5 * rw] = kk * a
    rw_ref[0, :, 5 * rw:6 * rw] = lw
    gb_ref[0, :, 0:rw] = g
    gb_ref[0, :, rw:2 * rw] = bonus


def _front_call(x, mod3, w_in, mu_shift, conv_w, conv_b, cg, cb, w0, a0, wa2, g2, kkw, kaw,
                rk, bd, tm):
    bsz, seq, dm = x.shape
    ksize, cw = conv_w.shape
    rw = w0.shape[-1]
    halo = 32
    assert ksize - 1 <= halo <= tm
    row = lambda a: a.reshape(1, -1)
    consts = [w_in, row(mu_shift), conv_w, row(conv_b), row(cg), row(cb), row(w0), row(a0),
              wa2, g2, row(kkw), row(kaw), row(rk), bd]
    kern = functools.partial(_front_kernel, dm=dm, cw=cw, rw=rw, ksize=ksize, halo=halo)
    return pl.pallas_call(
        kern,
        name="front",
        out_shape=(jax.ShapeDtypeStruct((bsz, seq, 6 * rw), F32),
                   jax.ShapeDtypeStruct((bsz, seq, cw), F32),
                   jax.ShapeDtypeStruct((bsz, seq, 2 * rw), F32)),
        grid=(bsz, seq // tm),
        in_specs=[pl.BlockSpec((1, tm, dm), lambda b, s: (b, s, 0)),
                  pl.BlockSpec((1, 1, mod3.shape[-1]), lambda b, s: (b, 0, 0))]
                 + [_const_spec(a.shape) for a in consts],
        out_specs=(pl.BlockSpec((1, tm, 6 * rw), lambda b, s: (b, s, 0)),
                   pl.BlockSpec((1, tm, cw), lambda b, s: (b, s, 0)),
                   pl.BlockSpec((1, tm, 2 * rw), lambda b, s: (b, s, 0))),
        scratch_shapes=[pltpu.VMEM((tm + halo, cw), F32),
                        pltpu.VMEM((8, w_in.shape[1] - 2 * cw), F32)],
        compiler_params=_cparams(("arbitrary", "arbitrary")),
    )(x, mod3, *consts)


SCAN_PASSES = 1
_NN = (((2,), (1,)), ((0,), (0,)))
_NT = (((2,), (2,)), ((0,), (0,)))
_TN = (((1,), (1,)), ((0,), (0,)))


def _split(x, passes=3):
    hi = x.astype(BF16)
    if passes == 1:
        return hi, None
    return hi, (x - hi.astype(F32)).astype(BF16)


def _mm3(a, b, dims):
    (ah, al), (bh, bl) = a, b
    dg = lambda p, q: lax.dot_general(p, q, dims, preferred_element_type=F32)
    out = dg(ah, bh)
    if bl is not None:
        out = out + dg(ah, bl)
    if al is not None:
        out = out + dg(al, bh)
    return out


def _scan_kernel(rw_ref, tri_ref, y_ref, state_ref, *, nh, hd):
    L = rw_ref.shape[1]
    rwid = nh * hd

    @pl.when(pl.program_id(1) == 0)
    def _():
        state_ref[...] = jnp.zeros(state_ref.shape, F32)

    nrow = rw_ref.shape[0]
    tri = tri_ref[...]
    parts = [[] for _ in range(6)]
    for b in range(nrow):
        blk = rw_ref[b]
        lw = blk[:, 5 * rwid:6 * rwid]
        c = _dot(tri, lw, HIGHEST)
        e_pos = jnp.exp(c)
        e_neg = jnp.exp(-c)
        parts[0].append(blk[:, 0:rwid] * e_pos)
        parts[1].append(blk[:, rwid:2 * rwid] * e_neg)
        parts[2].append(blk[:, 3 * rwid:4 * rwid] * jnp.exp(c - lw))
        parts[3].append(blk[:, 4 * rwid:5 * rwid] * e_neg)
        parts[4].append(blk[:, 2 * rwid:3 * rwid])
        parts[5].append(e_pos[L - 1:L, :])

    def heads(ts):
        return jnp.stack([t[:, h * hd:(h + 1) * hd] for t in ts for h in range(nh)], axis=0)

    rt, kh, kt, bh, v, plh = (heads(p) for p in parts)
    s0 = state_ref[...]

    strict = (tri - jnp.eye(L, dtype=F32))[None]
    incl = tri[None]
    sp = functools.partial(_split, passes=SCAN_PASSES)
    kt_s, bh_s, kh_s, rt_s, v_s, s0_s = (sp(t) for t in (kt, bh, kh, rt, v, s0))
    t_bb = _mm3(kt_s, bh_s, _NT) * strict
    t_bk = _mm3(kt_s, kh_s, _NT) * strict
    g_rb = _mm3(rt_s, bh_s, _NT) * incl
    g_rk = _mm3(rt_s, kh_s, _NT) * incl

    w = _mm3(kt_s, s0_s, _NT) + _mm3(sp(t_bk), v_s, _NN)
    tp_s = sp(t_bb)
    w = w - _mm3(tp_s, sp(w), _NN)
    n = 2
    while n < L:
        tp_s = sp(_mm3(tp_s, tp_s, _NN))
        w = w + _mm3(tp_s, sp(w), _NN)
        n *= 2
    u_s = sp(w)
    y = (_mm3(rt_s, s0_s, _NT) - _mm3(sp(g_rb), u_s, _NN)
         + _mm3(sp(g_rk), v_s, _NN))
    state_ref[...] = (s0 * plh + _mm3(v_s, sp(kh * plh), _TN)
                      - _mm3(u_s, sp(bh * plh), _TN))
    for b in range(nrow):
        y_ref[b] = jnp.concatenate([y[b * nh + h] for h in range(nh)], axis=-1)


def _scan_call(rwpack, nh, hd, chunk):
    bsz, seq, w6 = rwpack.shape
    rwid = nh * hd
    nrow = SCAN_ROWS if bsz % SCAN_ROWS == 0 else 1
    tri = jnp.tril(jnp.ones((chunk, chunk), F32))
    kern = functools.partial(_scan_kernel, nh=nh, hd=hd)
    return pl.pallas_call(
        kern,
        name="rwkv_scan",
        out_shape=jax.ShapeDtypeStruct((bsz, seq, rwid), F32),
        grid=(bsz // nrow, seq // chunk),
        in_specs=[pl.BlockSpec((nrow, chunk, w6), lambda b, s: (b, s, 0)),
                  _const_spec(tri.shape)],
        out_specs=pl.BlockSpec((nrow, chunk, rwid), lambda b, s: (b, s, 0)),
        scratch_shapes=[pltpu.VMEM((nrow * nh, hd, hd), F32)],
        compiler_params=_cparams(("arbitrary", "arbitrary")),
    )(rwpack, tri)


def _extract_max(s, iota, fill, remove=True):
    m = jnp.max(s, axis=0, keepdims=True)
    idx = jnp.min(jnp.where(s == m, iota, fill), axis=0, keepdims=True)
    return m, idx, (jnp.where(iota == idx, -jnp.inf, s) if remove else s)


def _topk_cols(s, iota, fill, k):
    vals, idxs = [], []
    for r in range(k):
        m, idx, s = _extract_max(s, iota, fill, remove=r + 1 < k)
        vals.append(m)
        idxs.append(idx)
    return jnp.concatenate(vals, axis=0), jnp.concatenate(idxs, axis=0)


def _topk_pair_sums(v1, v2, k):
    rows = lax.broadcasted_iota(jnp.int32, v1.shape, 0)
    base = rows * k
    ptr = jnp.zeros(v1.shape, jnp.int32)
    heads = v1 + v2[0:1, :]
    vals, picks = [], []
    for r in range(k):
        flat = base + ptr
        m, idx, _ = _extract_max(heads, flat, k * k, remove=False)
        vals.append(m)
        picks.append(idx)
        if r + 1 < k:
            hit = flat == idx
            nb = (idx & (k - 1)) + 1
            nxt = jnp.sum(jnp.where(rows == nb, v2, 0.0), axis=0, keepdims=True)
            nxt = jnp.where(nb < k, nxt, -jnp.inf)
            heads = jnp.where(hit, v1 + nxt, heads)
            ptr = jnp.where(hit, ptr + 1, ptr)
    return jnp.concatenate(vals, axis=0), jnp.concatenate(picks, axis=0)


def _take_rows(table, idx):
    out = jnp.zeros(idx.shape, table.dtype)
    for a in range(table.shape[0]):
        out = jnp.where(idx == a, table[a:a + 1, :], out)
    return out


def _post_kernel(x_ref, y_ref, yconv_ref, gb_ref, mod_ref, wout_ref, lnxg_ref, lnxb_ref,
                 ln1g_ref, ln1b_ref, wq_ref, k1_ref, k2_ref, bd_ref,
                 x1_ref, h2_ref, e_ref, gate_ref, q_ref, *, dm, rw, hd, alpha, nkeys, topk):
    tm = x_ref.shape[1]
    nheads = k1_ref.shape[0]
    half = k1_ref.shape[2]
    x = x_ref[0]
    mod = mod_ref[0]
    gate1 = mod[:, 2 * dm:3 * dm]
    shift2 = mod[:, 3 * dm:4 * dm]
    scale2 = mod[:, 4 * dm:5 * dm]

    y = y_ref[0]
    bd = bd_ref[...]
    mu = _head_sums(y, bd) * (1.0 / hd)
    yc = y - mu
    var = _head_sums(yc * yc, bd) * (1.0 / hd)
    gn = yc * lax.rsqrt(var + GN_EPS)
    gb = gb_ref[0]
    y_rw = (gn * lnxg_ref[...] + lnxb_ref[...] + gb[:, rw:2 * rw]) * gb[:, 0:rw]
    ycat = jnp.concatenate([yconv_ref[0], y_rw], axis=-1)
    y1 = _dot(ycat.astype(BF16), wout_ref[...])
    x1 = _ln(alpha * x + gate1 * y1, LN_EPS) * ln1g_ref[...] + ln1b_ref[...]
    x1_ref[0] = x1
    h2 = _ln(x1, LN_EPS) * (1.0 + scale2) + shift2
    h2_ref[0] = h2
    q = _dot(h2.astype(BF16), wq_ref[...])
    for i in range(2 * nheads):
        q_ref[i] = q[:, i * half:(i + 1) * half]

    shift = topk.bit_length() - 1
    assert topk == 1 << shift
    key_iota = lax.broadcasted_iota(jnp.int32, (nkeys, LANES), 0)

    def route(h, c0):
        s1 = lax.dot_general(k1_ref[h], q_ref[2 * h, c0:c0 + LANES, :], (((1,), (1,)), ((), ())),
                             preferred_element_type=F32, precision=HIGHEST)
        s2 = lax.dot_general(k2_ref[h], q_ref[2 * h + 1, c0:c0 + LANES, :],
                             (((1,), (1,)), ((), ())),
                             preferred_element_type=F32, precision=HIGHEST)
        v1, i1 = _topk_cols(s1, key_iota, nkeys, topk)
        v2, i2 = _topk_cols(s2, key_iota, nkeys, topk)
        sc, pick = _topk_pair_sums(v1, v2, topk)
        ex = (_take_rows(i1, lax.shift_right_logical(pick, shift)) * nkeys
              + _take_rows(i2, pick & (topk - 1)))
        pexp = jnp.exp(sc - sc[0:1, :])
        gates = pexp / jnp.sum(pexp, axis=0, keepdims=True)
        base = pl.multiple_of(h * topk, topk)
        e_ref[pl.ds(base, topk), c0:c0 + LANES] = ex
        gate_ref[pl.ds(base, topk), c0:c0 + LANES] = gates

    def head_body(hp, carry):
        for dh in range(ROUTE_HEADS_PER_TRIP):
            for c0 in range(0, tm, LANES):
                route(ROUTE_HEADS_PER_TRIP * hp + dh, c0)
        return carry

    lax.fori_loop(0, nheads // ROUTE_HEADS_PER_TRIP, head_body, 0)


def _post_call(x, yscan, yconv, gb, mod3, w_out, lnxg, lnxb, ln1g, ln1b, wq, k1, k2, bd,
               hd, alpha, topk, tm):
    bsz, seq, dm = x.shape
    rw = yscan.shape[-1]
    cw = yconv.shape[-1]
    nheads, nkeys, half = k1.shape
    nslot = nheads * topk
    ntok = bsz * seq
    nst = seq // tm
    row = lambda a: a.reshape(1, -1)
    consts = [w_out, row(lnxg), row(lnxb), row(ln1g), row(ln1b), wq, k1, k2, bd]
    kern = functools.partial(_post_kernel, dm=dm, rw=rw, hd=hd, alpha=alpha, nkeys=nkeys,
                             topk=topk)
    tok = lambda w: pl.BlockSpec((1, tm, w), lambda b, s: (b, s, 0))
    slot = pl.BlockSpec((nslot, tm), lambda b, s: (0, b * nst + s))
    return pl.pallas_call(
        kern,
        name="post_route",
        out_shape=(jax.ShapeDtypeStruct((bsz, seq, dm), F32),
                   jax.ShapeDtypeStruct((bsz, seq, dm), F32),
                   jax.ShapeDtypeStruct((nslot, ntok), jnp.int32),
                   jax.ShapeDtypeStruct((nslot, ntok), F32)),
        grid=(bsz, nst),
        in_specs=[tok(dm), tok(rw), tok(cw), tok(2 * rw),
                  pl.BlockSpec((1, 1, mod3.shape[-1]), lambda b, s: (b, 0, 0))]
                 + [_const_spec(a.shape) for a in consts],
        out_specs=(tok(dm), tok(dm), slot, slot),
        scratch_shapes=[pltpu.VMEM((2 * nheads, tm, half), F32)],
        compiler_params=_cparams(("arbitrary", "arbitrary")),
    )(x, yscan, yconv, gb, mod3, *consts)


def _pack_kernel(t_ref, o_ref):
    t = t_ref[...]
    hw = t.shape[1] // 2
    hi = lax.bitcast_convert_type(t[:, 0:hw].astype(BF16).astype(F32), jnp.uint32)
    lo = lax.bitcast_convert_type(t[:, hw:].astype(BF16).astype(F32), jnp.uint32)
    o_ref[...] = (hi & jnp.uint32(0xFFFF0000)) | (lo >> 16)


def _pack_call(table):
    ne, dm = table.shape
    packed = pl.pallas_call(
        _pack_kernel,
        name="peer_pack",
        out_shape=jax.ShapeDtypeStruct((ne, dm // 2), jnp.uint32),
        grid=(ne // PACK_ROWS,),
        in_specs=[pl.BlockSpec((PACK_ROWS, dm), lambda i: (i, 0))],
        out_specs=pl.BlockSpec((PACK_ROWS, dm // 2), lambda i: (i, 0)),
        compiler_params=_cparams(("arbitrary",)),
    )(table)
    return packed.reshape(ne * (dm // (2 * LANES)), LANES)


def _unpack(w):
    hi = lax.bitcast_convert_type(w & jnp.uint32(0xFFFF0000), F32)
    lo = lax.bitcast_convert_type(w << 16, F32)
    return hi, lo


def _gather_pair(tbl_ref, e_ref, off, nj):
    rows = [tbl_ref[pl.ds(pl.multiple_of(e_ref[off + d], nj), nj), :] for d in range(2)]
    return jnp.concatenate(rows, axis=0)


def _stack_chunks(row, first, nj):
    sub = lax.broadcasted_iota(jnp.int32, (2 * nj, LANES), 0) % nj
    chunk = lambda c: jnp.broadcast_to(row[:, c * LANES:(c + 1) * LANES], (2 * nj, LANES))
    out = chunk(first)
    for q in range(1, nj):
        out = jnp.where(sub == q, chunk(first + q), out)
    return out


def _peer_u_kernel(e_ref, h_ref, tbl_ref, z_ref, p0_ref, p1_ref, *, nslot, nj):
    tm = h_ref.shape[0]
    rows = 2 * nj
    lane = lax.broadcasted_iota(jnp.int32, (rows, LANES), 1)
    z_ref[...] = jnp.zeros(z_ref.shape, F32)

    @pl.when(pl.program_id(0) == 0)
    def _():
        p1_ref[...] = jnp.zeros(p1_ref.shape, F32)

    def reduce_rows(p_ref, row, tcol):
        rs = jnp.sum(p_ref[row:row + rows, :], axis=1, keepdims=True)
        z_ref[row:row + rows, :] = jnp.where(lane == tcol, rs, z_ref[row:row + rows, :])

    def one_token(t, p_ref, q_ref):
        hrow = h_ref[pl.ds(t, 1), :]
        h_hi = _stack_chunks(hrow, 0, nj)
        h_lo = _stack_chunks(hrow, nj, nj)
        ev = e_ref.at[pl.ds(t * nslot, nslot)]
        for i in range(0, nslot, 2):
            hi, lo = _unpack(_gather_pair(tbl_ref, ev, i, nj))
            p_ref[i * nj:i * nj + rows, :] = hi * h_hi + lo * h_lo
            reduce_rows(q_ref, i * nj, t - 1)

    def tok_body(i, carry):
        for d in range(0, U_TOKENS_PER_TRIP, 2):
            one_token(U_TOKENS_PER_TRIP * i + d, p0_ref, p1_ref)
            one_token(U_TOKENS_PER_TRIP * i + d + 1, p1_ref, p0_ref)
        return carry

    lax.fori_loop(0, tm // U_TOKENS_PER_TRIP, tok_body, 0)
    for r in range(0, nslot * nj, rows):
        reduce_rows(p1_ref, r, tm - 1)


def _peer_u_call(experts, h2, tbl, tm):
    ntok, dm = h2.shape
    nslot = experts.shape[0] // ntok
    nj = dm // (2 * LANES)
    kern = functools.partial(_peer_u_kernel, nslot=nslot, nj=nj)
    return pl.pallas_call(
        kern,
        name="peer_u",
        out_shape=jax.ShapeDtypeStruct((nslot * nj, ntok), F32),
        grid=(ntok // tm,),
        in_specs=[pl.BlockSpec((tm * nslot,), lambda i: (i,), memory_space=pltpu.SMEM),
                  pl.BlockSpec((tm, dm), lambda i: (i, 0)),
                  _const_spec(tbl.shape)],
        out_specs=pl.BlockSpec((nslot * nj, tm), lambda i: (0, i)),
        scratch_shapes=[pltpu.VMEM((nslot * nj, LANES), F32),
                        pltpu.VMEM((nslot * nj, LANES), F32)],
        compiler_params=_cparams(("arbitrary",)),
    )(experts, h2, tbl)


def _act_kernel(z_ref, sel_ref, gate_ref, a_ref):
    zp = z_ref[...]
    zh = zp.astype(BF16)
    zl = (zp - zh.astype(F32)).astype(BF16)
    sel = sel_ref[...]
    z = _dot(sel, zh) + _dot(sel, zl)
    a = 0.5 * z * (1.0 + lax.erf(z * (1.0 / math.sqrt(2.0)))) * gate_ref[...]
    a_ref[...] = a.T


def _act_call(zpart, gates, tm):
    nslot, ntok = gates.shape
    nj = zpart.shape[0] // nslot
    sel = (jnp.arange(nslot * nj)[None, :] // nj == jnp.arange(nslot)[:, None]).astype(BF16)
    return pl.pallas_call(
        _act_kernel,
        name="peer_act",
        out_shape=jax.ShapeDtypeStruct((ntok, nslot), F32),
        grid=(ntok // tm,),
        in_specs=[pl.BlockSpec((nslot * nj, tm), lambda i: (0, i)),
                  _const_spec(sel.shape),
                  pl.BlockSpec((nslot, tm), lambda i: (0, i))],
        out_specs=pl.BlockSpec((tm, nslot), lambda i: (i, 0)),
        compiler_params=_cparams(("arbitrary",)),
    )(zpart, sel, gates)


def _peer_v_kernel(e_ref, act_ref, v_ref, x_ref, mod_ref, g_ref, b_ref, o_ref, a3_ref, a2_ref,
                   *, nslot, nkeys, alpha):
    tm, dm = o_ref.shape
    shift = nkeys.bit_length() - 1
    sub_i1 = lax.broadcasted_iota(jnp.int32, (nkeys, nslot), 0)
    lane_i2 = lax.broadcasted_iota(jnp.int32, (nslot, nkeys), 1)

    def one_token(t):
        e_row = e_ref[pl.ds(t, 1), :]
        w_row = act_ref[pl.ds(t, 1), :]
        p1w = jnp.where(sub_i1 == lax.shift_right_logical(e_row, shift), w_row, 0.0)
        e_col = jnp.broadcast_to(e_row.astype(F32), (nslot, nslot)).T.astype(jnp.int32)
        p2 = jnp.where(lane_i2 == (e_col & (nkeys - 1)), 1.0, 0.0)
        row = pl.multiple_of(t * TILE_PITCH, SUBLANES)
        a3_ref[pl.ds(row, nkeys), :] = _dot(p1w.astype(BF16), p2.astype(BF16))

    def tok_body(i, carry):
        for d in range(V_TOKENS_PER_TRIP):
            one_token(V_TOKENS_PER_TRIP * i + d)
        return carry

    lax.fori_loop(0, tm // V_TOKENS_PER_TRIP, tok_body, 0)

    for r in range(0, tm, BF16_ROWS):
        for i1 in range(nkeys):
            rows = a3_ref[pl.ds(r * TILE_PITCH + i1, BF16_ROWS, stride=TILE_PITCH), :]
            a2_ref[r:r + BF16_ROWS, i1 * nkeys:(i1 + 1) * nkeys] = rows.astype(BF16)
    y = _dot(a2_ref[...], v_ref[...])
    gate2 = mod_ref[0][:, 5 * dm:6 * dm]
    o_ref[...] = _ln(alpha * x_ref[...] + gate2 * y, LN_EPS) * g_ref[...] + b_ref[...]


def _peer_v_call(experts, act, table, x1, mod3, g, b, alpha, nkeys, seq, tm):
    ntok, nslot = act.shape
    ne, dm = table.shape
    assert nslot == LANES and ne == nkeys * nkeys and seq % tm == 0 and TILE_PITCH >= nkeys
    kern = functools.partial(_peer_v_kernel, nslot=nslot, nkeys=nkeys, alpha=alpha)
    tok = pl.BlockSpec((tm, dm), lambda i: (i, 0))
    slots = pl.BlockSpec((tm, nslot), lambda i: (i, 0))
    return pl.pallas_call(
        kern,
        name="peer_v",
        out_shape=jax.ShapeDtypeStruct((ntok, dm), F32),
        grid=(ntok // tm,),
        in_specs=[slots, slots, _const_spec(table.shape), tok,
                  pl.BlockSpec((1, 1, mod3.shape[-1]), lambda i: (i * tm // seq, 0, 0)),
                  _const_spec((1, dm)), _const_spec((1, dm))],
        out_specs=tok,
        scratch_shapes=[pltpu.VMEM((tm * TILE_PITCH, nkeys), F32), pltpu.VMEM((tm, ne), BF16)],
        compiler_params=_cparams(("arbitrary",)),
    )(experts, act, table, x1, mod3, g.reshape(1, -1), b.reshape(1, -1))


def kernel(x, c, cond_w, cond_b, w_in, mu_shift, conv_w, conv_b, conv_ln_g, conv_ln_b, rw_w0, rw_w2, rw_a0, rw_a2, rw_g2, rw_kk, rw_ka, rw_rk, rw_lnx_g, rw_lnx_b, w_out, ln1_g, ln1_b, peer_wq, peer_k1, peer_k2, peer_u, peer_v, ln2_g, ln2_b):
    bsz, seq, dm = x.shape
    nh, hd = rw_rk.shape
    rw = nh * hd
    lora_w, lora_a = rw_w2.shape[0], rw_a2.shape[0]
    assert lora_w == lora_a == LANES // 2 and rw_g2.shape[0] == LANES
    topk = PEER_TOPK
    alpha = (2.0 * DEPTH) ** 0.25
    tm = min(ROW_TILE, seq)
    chunk = min(SCAN_CHUNK, seq)

    mod3 = _mod_call(c, cond_w, cond_b).reshape(bsz, 1, -1)

    head_id = jnp.arange(rw) // hd
    bd = (head_id[:, None] == head_id[None, :]).astype(BF16)
    wa2 = jnp.zeros((LANES, 2 * rw), F32)
    wa2 = wa2.at[0:lora_w, 0:rw].set(rw_w2).at[lora_w:, rw:].set(rw_a2)

    rwpack, yconv, gb = _front_call(
        x, mod3, w_in.astype(BF16), mu_shift, conv_w, conv_b, conv_ln_g, conv_ln_b, rw_w0,
        rw_a0, wa2, rw_g2, rw_kk, rw_ka, rw_rk.reshape(-1), bd, min(FRONT_TILE, seq))
    yscan = _scan_call(rwpack, nh, hd, chunk)
    x1, h2, experts, gates = _post_call(
        x, yscan, yconv, gb, mod3, w_out.astype(BF16), rw_lnx_g, rw_lnx_b, ln1_g, ln1_b,
        peer_wq.astype(BF16), peer_k1, peer_k2, bd, hd, alpha, topk, tm)

    ntok = bsz * seq
    ptm = min(PEER_TOK, seq)
    nj = dm // (2 * LANES)
    experts_t = experts.T
    offsets = (experts_t * nj).reshape(-1)
    zpart = _peer_u_call(offsets, h2.reshape(ntok, dm), _pack_call(peer_u), ptm)
    act = _act_call(zpart, gates, ptm)
    out = _peer_v_call(experts_t, act, peer_v.astype(BF16), x1.reshape(ntok, dm), mod3,
                       ln2_g, ln2_b, alpha, peer_k1.shape[1], seq, ptm)
    return out.reshape(bsz, seq, dm)
```

```python
import functools
import math

import jax
import jax.numpy as jnp
from jax import lax
from jax.experimental import pallas as pl
from jax.experimental.pallas import tpu as pltpu

F32 = jnp.float32
BF16 = jnp.bfloat16
HIGHEST = lax.Precision.HIGHEST

LN_EPS = 1e-5
GN_EPS = 64e-5
PEER_TOPK = 16
DEPTH = 1
LANES = 128
SUBLANES = 8
ROW_TILE = 256
FRONT_TILE = 256
ROUTE_HEADS_PER_TRIP = 4
SCAN_CHUNK = 64
SCAN_ROWS = 4
PEER_TOK = 128
U_TOKENS_PER_TRIP = 32
V_TOKENS_PER_TRIP = 32
TILE_PITCH = 136
BF16_ROWS = 16
PACK_ROWS = 512
VMEM_LIMIT = 56 * 1024 * 1024


def _cparams(sem):
    return pltpu.CompilerParams(dimension_semantics=sem, vmem_limit_bytes=VMEM_LIMIT)


def _dot(a, b, precision=None):
    return jnp.dot(a, b, preferred_element_type=F32, precision=precision)


def _sigmoid(x):
    return 1.0 / (1.0 + jnp.exp(-x))


def _ln(x, eps):
    mu = jnp.mean(x, axis=-1, keepdims=True)
    xc = x - mu
    var = jnp.mean(xc * xc, axis=-1, keepdims=True)
    return xc * lax.rsqrt(var + eps)


def _head_sums(x, bd):
    hi = x.astype(BF16)
    lo = (x - hi.astype(F32)).astype(BF16)
    return _dot(hi, bd) + _dot(lo, bd)


def _const_spec(shape):
    nd = len(shape)
    return pl.BlockSpec(shape, lambda *_: (0,) * nd, pipeline_mode=pl.Buffered(1))


def _mod_kernel(c_ref, w_ref, b_ref, o_ref):
    c = c_ref[...]
    o_ref[...] = _dot(c * _sigmoid(c), w_ref[...], HIGHEST) + b_ref[...]


def _mod_call(c, cond_w, cond_b):
    bsz, dm = c.shape
    nblk = cond_w.shape[1] // dm
    return pl.pallas_call(
        _mod_kernel,
        name="mod",
        out_shape=jax.ShapeDtypeStruct((bsz, nblk * dm), F32),
        grid=(nblk,),
        in_specs=[pl.BlockSpec((bsz, dm), lambda j: (0, 0)),
                  pl.BlockSpec((dm, dm), lambda j: (0, j)),
                  pl.BlockSpec((1, dm), lambda j: (0, j))],
        out_specs=pl.BlockSpec((bsz, dm), lambda j: (0, j)),
        compiler_params=_cparams(("arbitrary",)),
    )(c, cond_w, cond_b.reshape(1, -1))


def _front_kernel(x_ref, mod_ref, win_ref, mu_ref, convw_ref, convb_ref, cg_ref, cb_ref,
                  w0_ref, a0_ref, wa2_ref, g2_ref, kkw_ref, kaw_ref, rk_ref, bd_ref,
                  rw_ref, yconv_ref, gb_ref, ubuf_ref, prev_ref, *, dm, cw, rw, ksize, halo):
    tm = x_ref.shape[1]
    first = pl.program_id(1) == 0

    @pl.when(first)
    def _():
        ubuf_ref[0:halo, :] = jnp.zeros((halo, cw), F32)
        prev_ref[...] = jnp.zeros(prev_ref.shape, F32)

    x = x_ref[0]
    mod = mod_ref[0]
    shift1 = mod[:, 0:dm]
    scale1 = mod[:, dm:2 * dm]
    h = _ln(x, LN_EPS) * (1.0 + scale1) + shift1
    p = _dot(h.astype(BF16), win_ref[...])

    u = p[:, 0:cw] * _sigmoid(p[:, cw:2 * cw])
    ubuf_ref[halo:halo + tm, :] = u
    acc = jnp.zeros((tm, cw), F32) + convb_ref[...]
    off = halo - (ksize - 1)
    ub = ubuf_ref[...]
    nrow = tm + halo
    for b in range(SUBLANES):
        xb = ub if b == 0 else pltpu.roll(ub, nrow - b, 0)
        for j in range(ksize):
            if (off + j) % SUBLANES == b:
                a8 = off + j - b
                acc = acc + convw_ref[j:j + 1, :] * xb[a8:a8 + tm, :]
    ubuf_ref[0:halo, :] = ubuf_ref[tm:tm + halo, :]
    yc = _ln(acc, LN_EPS) * cg_ref[...] + cb_ref[...]
    yconv_ref[0] = yc * _sigmoid(yc)

    prw = p[:, 2 * cw:]
    rolled = pltpu.roll(prw, 1, 0)
    row = lax.broadcasted_iota(jnp.int32, prw.shape, 0)
    p_prev = jnp.where(row == 0, prev_ref[0:1, :], rolled)
    prev_ref[0:1, :] = prw[tm - 1:tm, :]
    xm = prw + (p_prev - prw) * mu_ref[...]
    r = xm[:, 0:rw]
    k = xm[:, rw:2 * rw]
    v = xm[:, 2 * rw:3 * rw]
    wa = xm[:, 3 * rw:3 * rw + LANES]
    gd = xm[:, 3 * rw + LANES:3 * rw + 2 * LANES]
    lane = lax.broadcasted_iota(jnp.int32, wa.shape, 1)
    wa = jnp.where(lane < LANES // 2, jnp.tanh(wa), wa)
    t12 = _dot(wa, wa2_ref[...], HIGHEST)
    g = _dot(_sigmoid(gd), g2_ref[...], HIGHEST)
    y = w0_ref[...] + t12[:, 0:rw]
    sp = jnp.maximum(-y, 0.0) + jnp.log(1.0 + jnp.exp(-jnp.abs(y)))
    lw = -jnp.exp(-sp - 0.5)
    a = _sigmoid(a0_ref[...] + t12[:, rw:2 * rw])
    kk = k * kkw_ref[...]
    ss = _head_sums(kk * kk, bd_ref[...])
    kk = kk / jnp.maximum(jnp.sqrt(ss), 1e-12)
    kmod = k * (1.0 + (a - 1.0) * kaw_ref[...])
    bonus = _head_sums(r * kmod * rk_ref[...], bd_ref[...]) * v
    rw_ref[0, :, 0:rw] = r
    rw_ref[0, :, rw:2 * rw] = kmod
    rw_ref[0, :, 2 * rw:3 * rw] = v
    rw_ref[0, :, 3 * rw:4 * rw] = kk
    rw_ref[0, :, 4 * rw:5 * rw] = kk * a
    rw_ref[0, :, 5 * rw:6 * rw] = lw
    gb_ref[0, :, 0:rw] = g
    gb_ref[0, :, rw:2 * rw] = bonus


def _front_call(x, mod3, w_in, mu_shift, conv_w, conv_b, cg, cb, w0, a0, wa2, g2, kkw, kaw,
                rk, bd, tm):
    bsz, seq, dm = x.shape
    ksize, cw = conv_w.shape
    rw = w0.shape[-1]
    halo = 32
    assert ksize - 1 <= halo <= tm
    row = lambda a: a.reshape(1, -1)
    consts = [w_in, row(mu_shift), conv_w, row(conv_b), row(cg), row(cb), row(w0), row(a0),
              wa2, g2, row(kkw), row(kaw), row(rk), bd]
    kern = functools.partial(_front_kernel, dm=dm, cw=cw, rw=rw, ksize=ksize, halo=halo)
    return pl.pallas_call(
        kern,
        name="front",
        out_shape=(jax.ShapeDtypeStruct((bsz, seq, 6 * rw), F32),
                   jax.ShapeDtypeStruct((bsz, seq, cw), F32),
                   jax.ShapeDtypeStruct((bsz, seq, 2 * rw), F32)),
        grid=(bsz, seq // tm),
        in_specs=[pl.BlockSpec((1, tm, dm), lambda b, s: (b, s, 0)),
                  pl.BlockSpec((1, 1, mod3.shape[-1]), lambda b, s: (b, 0, 0))]
                 + [_const_spec(a.shape) for a in consts],
        out_specs=(pl.BlockSpec((1, tm, 6 * rw), lambda b, s: (b, s, 0)),
                   pl.BlockSpec((1, tm, cw), lambda b, s: (b, s, 0)),
                   pl.BlockSpec((1, tm, 2 * rw), lambda b, s: (b, s, 0))),
        scratch_shapes=[pltpu.VMEM((tm + halo, cw), F32),
                        pltpu.VMEM((8, w_in.shape[1] - 2 * cw), F32)],
        compiler_params=_cparams(("arbitrary", "arbitrary")),
    )(x, mod3, *consts)


SCAN_PASSES = 1
_NN = (((2,), (1,)), ((0,), (0,)))
_NT = (((2,), (2,)), ((0,), (0,)))
_TN = (((1,), (1,)), ((0,), (0,)))


def _split(x, passes=3):
    hi = x.astype(BF16)
    if passes == 1:
        return hi, None
    return hi, (x - hi.astype(F32)).astype(BF16)


def _mm3(a, b, dims):
    (ah, al), (bh, bl) = a, b
    dg = lambda p, q: lax.dot_general(p, q, dims, preferred_element_type=F32)
    out = dg(ah, bh)
    if bl is not None:
        out = out + dg(ah, bl)
    if al is not None:
        out = out + dg(al, bh)
    return out


def _scan_kernel(rw_ref, tri_ref, y_ref, state_ref, *, nh, hd):
    L = rw_ref.shape[1]
    rwid = nh * hd

    @pl.when(pl.program_id(1) == 0)
    def _():
        state_ref[...] = jnp.zeros(state_ref.shape, F32)

    nrow = rw_ref.shape[0]
    tri = tri_ref[...]
    parts = [[] for _ in range(6)]
    for b in range(nrow):
        blk = rw_ref[b]
        lw = blk[:, 5 * rwid:6 * rwid]
        c = _dot(tri, lw, HIGHEST)
        e_pos = jnp.exp(c)
        e_neg = jnp.exp(-c)
        parts[0].append(blk[:, 0:rwid] * e_pos)
        parts[1].append(blk[:, rwid:2 * rwid] * e_neg)
        parts[2].append(blk[:, 3 * rwid:4 * rwid] * jnp.exp(c - lw))
        parts[3].append(blk[:, 4 * rwid:5 * rwid] * e_neg)
        parts[4].append(blk[:, 2 * rwid:3 * rwid])
        parts[5].append(e_pos[L - 1:L, :])

    def heads(ts):
        return jnp.stack([t[:, h * hd:(h + 1) * hd] for t in ts for h in range(nh)], axis=0)

    rt, kh, kt, bh, v, plh = (heads(p) for p in parts)
    s0 = state_ref[...]

    strict = (tri - jnp.eye(L, dtype=F32))[None]
    incl = tri[None]
    sp = functools.partial(_split, passes=SCAN_PASSES)
    kt_s, bh_s, kh_s, rt_s, v_s, s0_s = (sp(t) for t in (kt, bh, kh, rt, v, s0))
    t_bb = _mm3(kt_s, bh_s, _NT) * strict
    t_bk = _mm3(kt_s, kh_s, _NT) * strict
    g_rb = _mm3(rt_s, bh_s, _NT) * incl
    g_rk = _mm3(rt_s, kh_s, _NT) * incl

    w = _mm3(kt_s, s0_s, _NT) + _mm3(sp(t_bk), v_s, _NN)
    tp_s = sp(t_bb)
    w = w - _mm3(tp_s, sp(w), _NN)
    n = 2
    while n < L:
        tp_s = sp(_mm3(tp_s, tp_s, _NN))
        w = w + _mm3(tp_s, sp(w), _NN)
        n *= 2
    u_s = sp(w)
    y = (_mm3(rt_s, s0_s, _NT) - _mm3(sp(g_rb), u_s, _NN)
         + _mm3(sp(g_rk), v_s, _NN))
    state_ref[...] = (s0 * plh + _mm3(v_s, sp(kh * plh), _TN)
                      - _mm3(u_s, sp(bh * plh), _TN))
    for b in range(nrow):
        y_ref[b] = jnp.concatenate([y[b * nh + h] for h in range(nh)], axis=-1)


def _scan_call(rwpack, nh, hd, chunk):
    bsz, seq, w6 = rwpack.shape
    rwid = nh * hd
    nrow = SCAN_ROWS if bsz % SCAN_ROWS == 0 else 1
    tri = jnp.tril(jnp.ones((chunk, chunk), F32))
    kern = functools.partial(_scan_kernel, nh=nh, hd=hd)
    return pl.pallas_call(
        kern,
        name="rwkv_scan",
        out_shape=jax.ShapeDtypeStruct((bsz, seq, rwid), F32),
        grid=(bsz // nrow, seq // chunk),
        in_specs=[pl.BlockSpec((nrow, chunk, w6), lambda b, s: (b, s, 0)),
                  _const_spec(tri.shape)],
        out_specs=pl.BlockSpec((nrow, chunk, rwid), lambda b, s: (b, s, 0)),
        scratch_shapes=[pltpu.VMEM((nrow * nh, hd, hd), F32)],
        compiler_params=_cparams(("arbitrary", "arbitrary")),
    )(rwpack, tri)


def _extract_max(s, iota, fill, remove=True):
    m = jnp.max(s, axis=0, keepdims=True)
    idx = jnp.min(jnp.where(s == m, iota, fill), axis=0, keepdims=True)
    return m, idx, (jnp.where(iota == idx, -jnp.inf, s) if remove else s)


def _topk_cols(s, iota, fill, k):
    vals, idxs = [], []
    for r in range(k):
        m, idx, s = _extract_max(s, iota, fill, remove=r + 1 < k)
        vals.append(m)
        idxs.append(idx)
    return jnp.concatenate(vals, axis=0), jnp.concatenate(idxs, axis=0)


def _topk_pair_sums(v1, v2, k):
    rows = lax.broadcasted_iota(jnp.int32, v1.shape, 0)
    base = rows * k
    ptr = jnp.zeros(v1.shape, jnp.int32)
    heads = v1 + v2[0:1, :]
    vals, picks = [], []
    for r in range(k):
        flat = base + ptr
        m, idx, _ = _extract_max(heads, flat, k * k, remove=False)
        vals.append(m)
        picks.append(idx)
        if r + 1 < k:
            hit = flat == idx
            nb = (idx & (k - 1)) + 1
            nxt = jnp.sum(jnp.where(rows == nb, v2, 0.0), axis=0, keepdims=True)
            nxt = jnp.where(nb < k, nxt, -jnp.inf)
            heads = jnp.where(hit, v1 + nxt, heads)
            ptr = jnp.where(hit, ptr + 1, ptr)
    return jnp.concatenate(vals, axis=0), jnp.concatenate(picks, axis=0)


def _take_rows(table, idx):
    out = jnp.zeros(idx.shape, table.dtype)
    for a in range(table.shape[0]):
        out = jnp.where(idx == a, table[a:a + 1, :], out)
    return out


def _post_kernel(x_ref, y_ref, yconv_ref, gb_ref, mod_ref, wout_ref, lnxg_ref, lnxb_ref,
                 ln1g_ref, ln1b_ref, wq_ref, k1_ref, k2_ref, bd_ref,
                 x1_ref, h2_ref, e_ref, gate_ref, q_ref, *, dm, rw, hd, alpha, nkeys, topk):
    tm = x_ref.shape[1]
    nheads = k1_ref.shape[0]
    half = k1_ref.shape[2]
    x = x_ref[0]
    mod = mod_ref[0]
    gate1 = mod[:, 2 * dm:3 * dm]
    shift2 = mod[:, 3 * dm:4 * dm]
    scale2 = mod[:, 4 * dm:5 * dm]

    y = y_ref[0]
    bd = bd_ref[...]
    mu = _head_sums(y, bd) * (1.0 / hd)
    yc = y - mu
    var = _head_sums(yc * yc, bd) * (1.0 / hd)
    gn = yc * lax.rsqrt(var + GN_EPS)
    gb = gb_ref[0]
    y_rw = (gn * lnxg_ref[...] + lnxb_ref[...] + gb[:, rw:2 * rw]) * gb[:, 0:rw]
    ycat = jnp.concatenate([yconv_ref[0], y_rw], axis=-1)
    y1 = _dot(ycat.astype(BF16), wout_ref[...])
    x1 = _ln(alpha * x + gate1 * y1, LN_EPS) * ln1g_ref[...] + ln1b_ref[...]
    x1_ref[0] = x1
    h2 = _ln(x1, LN_EPS) * (1.0 + scale2) + shift2
    h2_ref[0] = h2
    q = _dot(h2.astype(BF16), wq_ref[...])
    for i in range(2 * nheads):
        q_ref[i] = q[:, i * half:(i + 1) * half]

    shift = topk.bit_length() - 1
    assert topk == 1 << shift
    key_iota = lax.broadcasted_iota(jnp.int32, (nkeys, LANES), 0)

    def route(h, c0):
        s1 = lax.dot_general(k1_ref[h], q_ref[2 * h, c0:c0 + LANES, :], (((1,), (1,)), ((), ())),
                             preferred_element_type=F32, precision=HIGHEST)
        s2 = lax.dot_general(k2_ref[h], q_ref[2 * h + 1, c0:c0 + LANES, :],
                             (((1,), (1,)), ((), ())),
                             preferred_element_type=F32, precision=HIGHEST)
        v1, i1 = _topk_cols(s1, key_iota, nkeys, topk)
        v2, i2 = _topk_cols(s2, key_iota, nkeys, topk)
        sc, pick = _topk_pair_sums(v1, v2, topk)
        ex = (_take_rows(i1, lax.shift_right_logical(pick, shift)) * nkeys
              + _take_rows(i2, pick & (topk - 1)))
        pexp = jnp.exp(sc - sc[0:1, :])
        gates = pexp / jnp.sum(pexp, axis=0, keepdims=True)
        base = pl.multiple_of(h * topk, topk)
        e_ref[pl.ds(base, topk), c0:c0 + LANES] = ex
        gate_ref[pl.ds(base, topk), c0:c0 + LANES] = gates

    def head_body(hp, carry):
        for dh in range(ROUTE_HEADS_PER_TRIP):
            for c0 in range(0, tm, LANES):
                route(ROUTE_HEADS_PER_TRIP * hp + dh, c0)
        return carry

    lax.fori_loop(0, nheads // ROUTE_HEADS_PER_TRIP, head_body, 0)


def _post_call(x, yscan, yconv, gb, mod3, w_out, lnxg, lnxb, ln1g, ln1b, wq, k1, k2, bd,
               hd, alpha, topk, tm):
    bsz, seq, dm = x.shape
    rw = yscan.shape[-1]
    cw = yconv.shape[-1]
    nheads, nkeys, half = k1.shape
    nslot = nheads * topk
    ntok = bsz * seq
    nst = seq // tm
    row = lambda a: a.reshape(1, -1)
    consts = [w_out, row(lnxg), row(lnxb), row(ln1g), row(ln1b), wq, k1, k2, bd]
    kern = functools.partial(_post_kernel, dm=dm, rw=rw, hd=hd, alpha=alpha, nkeys=nkeys,
                             topk=topk)
    tok = lambda w: pl.BlockSpec((1, tm, w), lambda b, s: (b, s, 0))
    slot = pl.BlockSpec((nslot, tm), lambda b, s: (0, b * nst + s))
    return pl.pallas_call(
        kern,
        name="post_route",
        out_shape=(jax.ShapeDtypeStruct((bsz, seq, dm), F32),
                   jax.ShapeDtypeStruct((bsz, seq, dm), F32),
                   jax.ShapeDtypeStruct((nslot, ntok), jnp.int32),
                   jax.ShapeDtypeStruct((nslot, ntok), F32)),
        grid=(bsz, nst),
        in_specs=[tok(dm), tok(rw), tok(cw), tok(2 * rw),
                  pl.BlockSpec((1, 1, mod3.shape[-1]), lambda b, s: (b, 0, 0))]
                 + [_const_spec(a.shape) for a in consts],
        out_specs=(tok(dm), tok(dm), slot, slot),
        scratch_shapes=[pltpu.VMEM((2 * nheads, tm, half), F32)],
        compiler_params=_cparams(("arbitrary", "arbitrary")),
    )(x, yscan, yconv, gb, mod3, *consts)


def _pack_kernel(t_ref, o_ref):
    t = t_ref[...]
    hw = t.shape[1] // 2
    hi = lax.bitcast_convert_type(t[:, 0:hw].astype(BF16).astype(F32), jnp.uint32)
    lo = lax.bitcast_convert_type(t[:, hw:].astype(BF16).astype(F32), jnp.uint32)
    o_ref[...] = (hi & jnp.uint32(0xFFFF0000)) | (lo >> 16)


def _pack_call(table):
    ne, dm = table.shape
    packed = pl.pallas_call(
        _pack_kernel,
        name="peer_pack",
        out_shape=jax.ShapeDtypeStruct((ne, dm // 2), jnp.uint32),
        grid=(ne // PACK_ROWS,),
        in_specs=[pl.BlockSpec((PACK_ROWS, dm), lambda i: (i, 0))],
        out_specs=pl.BlockSpec((PACK_ROWS, dm // 2), lambda i: (i, 0)),
        compiler_params=_cparams(("arbitrary",)),
    )(table)
    return packed.reshape(ne * (dm // (2 * LANES)), LANES)


def _unpack(w):
    hi = lax.bitcast_convert_type(w & jnp.uint32(0xFFFF0000), F32)
    lo = lax.bitcast_convert_type(w << 16, F32)
    return hi, lo


def _gather_pair(tbl_ref, e_ref, off, nj):
    rows = [tbl_ref[pl.ds(pl.multiple_of(e_ref[off + d], nj), nj), :] for d in range(2)]
    return jnp.concatenate(rows, axis=0)


def _stack_chunks(row, first, nj):
    sub = lax.broadcasted_iota(jnp.int32, (2 * nj, LANES), 0) % nj
    chunk = lambda c: jnp.broadcast_to(row[:, c * LANES:(c + 1) * LANES], (2 * nj, LANES))
    out = chunk(first)
    for q in range(1, nj):
        out = jnp.where(sub == q, chunk(first + q), out)
    return out


def _peer_u_kernel(e_ref, h_ref, tbl_ref, z_ref, p0_ref, p1_ref, *, nslot, nj):
    tm = h_ref.shape[0]
    rows = 2 * nj
    lane = lax.broadcasted_iota(jnp.int32, (rows, LANES), 1)
    z_ref[...] = jnp.zeros(z_ref.shape, F32)

    @pl.when(pl.program_id(0) == 0)
    def _():
        p1_ref[...] = jnp.zeros(p1_ref.shape, F32)

    def reduce_rows(p_ref, row, tcol):
        rs = jnp.sum(p_ref[row:row + rows, :], axis=1, keepdims=True)
        z_ref[row:row + rows, :] = jnp.where(lane == tcol, rs, z_ref[row:row + rows, :])

    def one_token(t, p_ref, q_ref):
        hrow = h_ref[pl.ds(t, 1), :]
        h_hi = _stack_chunks(hrow, 0, nj)
        h_lo = _stack_chunks(hrow, nj, nj)
        ev = e_ref.at[pl.ds(t * nslot, nslot)]
        for i in range(0, nslot, 2):
            hi, lo = _unpack(_gather_pair(tbl_ref, ev, i, nj))
            p_ref[i * nj:i * nj + rows, :] = hi * h_hi + lo * h_lo
            reduce_rows(q_ref, i * nj, t - 1)

    def tok_body(i, carry):
        for d in range(0, U_TOKENS_PER_TRIP, 2):
            one_token(U_TOKENS_PER_TRIP * i + d, p0_ref, p1_ref)
            one_token(U_TOKENS_PER_TRIP * i + d + 1, p1_ref, p0_ref)
        return carry

    lax.fori_loop(0, tm // U_TOKENS_PER_TRIP, tok_body, 0)
    for r in range(0, nslot * nj, rows):
        reduce_rows(p1_ref, r, tm - 1)


def _peer_u_call(experts, h2, tbl, tm):
    ntok, dm = h2.shape
    nslot = experts.shape[0] // ntok
    nj = dm // (2 * LANES)
    kern = functools.partial(_peer_u_kernel, nslot=nslot, nj=nj)
    return pl.pallas_call(
        kern,
        name="peer_u",
        out_shape=jax.ShapeDtypeStruct((nslot * nj, ntok), F32),
        grid=(ntok // tm,),
        in_specs=[pl.BlockSpec((tm * nslot,), lambda i: (i,), memory_space=pltpu.SMEM),
                  pl.BlockSpec((tm, dm), lambda i: (i, 0)),
                  _const_spec(tbl.shape)],
        out_specs=pl.BlockSpec((nslot * nj, tm), lambda i: (0, i)),
        scratch_shapes=[pltpu.VMEM((nslot * nj, LANES), F32),
                        pltpu.VMEM((nslot * nj, LANES), F32)],
        compiler_params=_cparams(("arbitrary",)),
    )(experts, h2, tbl)


def _act_kernel(z_ref, sel_ref, gate_ref, a_ref):
    zp = z_ref[...]
    zh = zp.astype(BF16)
    zl = (zp - zh.astype(F32)).astype(BF16)
    sel = sel_ref[...]
    z = _dot(sel, zh) + _dot(sel, zl)
    a = 0.5 * z * (1.0 + lax.erf(z * (1.0 / math.sqrt(2.0)))) * gate_ref[...]
    a_ref[...] = a.T


def _act_call(zpart, gates, tm):
    nslot, ntok = gates.shape
    nj = zpart.shape[0] // nslot
    sel = (jnp.arange(nslot * nj)[None, :] // nj == jnp.arange(nslot)[:, None]).astype(BF16)
    return pl.pallas_call(
        _act_kernel,
        name="peer_act",
        out_shape=jax.ShapeDtypeStruct((ntok, nslot), F32),
        grid=(ntok // tm,),
        in_specs=[pl.BlockSpec((nslot * nj, tm), lambda i: (0, i)),
                  _const_spec(sel.shape),
                  pl.BlockSpec((nslot, tm), lambda i: (0, i))],
        out_specs=pl.BlockSpec((tm, nslot), lambda i: (i, 0)),
        compiler_params=_cparams(("arbitrary",)),
    )(zpart, sel, gates)


def _peer_v_kernel(e_ref, act_ref, v_ref, x_ref, mod_ref, g_ref, b_ref, o_ref, a3_ref, a2_ref,
                   *, nslot, nkeys, alpha):
    tm, dm = o_ref.shape
    shift = nkeys.bit_length() - 1
    sub_key = lax.broadcasted_iota(jnp.int32, (nkeys, nslot), 0)

    def one_token(t):
        e_row = e_ref[pl.ds(t, 1), :]
        w_row = act_ref[pl.ds(t, 1), :]
        p1w = jnp.where(sub_key == lax.shift_right_logical(e_row, shift), w_row, 0.0)
        p2t = jnp.where(sub_key == (e_row & (nkeys - 1)), 1.0, 0.0)
        row = pl.multiple_of(t * TILE_PITCH, SUBLANES)
        a3_ref[pl.ds(row, nkeys), :] = lax.dot_general(
            p1w.astype(BF16), p2t.astype(BF16), (((1,), (1,)), ((), ())),
            preferred_element_type=F32)

    def tok_body(i, carry):
        for d in range(V_TOKENS_PER_TRIP):
            one_token(V_TOKENS_PER_TRIP * i + d)
        return carry

    lax.fori_loop(0, tm // V_TOKENS_PER_TRIP, tok_body, 0)

    for r in range(0, tm, BF16_ROWS):
        for i1 in range(nkeys):
            rows = a3_ref[pl.ds(r * TILE_PITCH + i1, BF16_ROWS, stride=TILE_PITCH), :]
            a2_ref[r:r + BF16_ROWS, i1 * nkeys:(i1 + 1) * nkeys] = rows.astype(BF16)
    y = _dot(a2_ref[...], v_ref[...])
    gate2 = mod_ref[0][:, 5 * dm:6 * dm]
    o_ref[...] = _ln(alpha * x_ref[...] + gate2 * y, LN_EPS) * g_ref[...] + b_ref[...]


def _peer_v_call(experts, act, table, x1, mod3, g, b, alpha, nkeys, seq, tm):
    ntok, nslot = act.shape
    ne, dm = table.shape
    assert nslot == LANES and ne == nkeys * nkeys and seq % tm == 0 and TILE_PITCH >= nkeys
    kern = functools.partial(_peer_v_kernel, nslot=nslot, nkeys=nkeys, alpha=alpha)
    tok = pl.BlockSpec((tm, dm), lambda i: (i, 0))
    slots = pl.BlockSpec((tm, nslot), lambda i: (i, 0))
    return pl.pallas_call(
        kern,
        name="peer_v",
        out_shape=jax.ShapeDtypeStruct((ntok, dm), F32),
        grid=(ntok // tm,),
        in_specs=[slots, slots, _const_spec(table.shape), tok,
                  pl.BlockSpec((1, 1, mod3.shape[-1]), lambda i: (i * tm // seq, 0, 0)),
                  _const_spec((1, dm)), _const_spec((1, dm))],
        out_specs=tok,
        scratch_shapes=[pltpu.VMEM((tm * TILE_PITCH, nkeys), F32), pltpu.VMEM((tm, ne), BF16)],
        compiler_params=_cparams(("arbitrary",)),
    )(experts, act, table, x1, mod3, g.reshape(1, -1), b.reshape(1, -1))


def kernel(x, c, cond_w, cond_b, w_in, mu_shift, conv_w, conv_b, conv_ln_g, conv_ln_b, rw_w0, rw_w2, rw_a0, rw_a2, rw_g2, rw_kk, rw_ka, rw_rk, rw_lnx_g, rw_lnx_b, w_out, ln1_g, ln1_b, peer_wq, peer_k1, peer_k2, peer_u, peer_v, ln2_g, ln2_b):
    bsz, seq, dm = x.shape
    nh, hd = rw_rk.shape
    rw = nh * hd
    lora_w, lora_a = rw_w2.shape[0], rw_a2.shape[0]
    assert lora_w == lora_a == LANES // 2 and rw_g2.shape[0] == LANES
    topk = PEER_TOPK
    alpha = (2.0 * DEPTH) ** 0.25
    tm = min(ROW_TILE, seq)
    chunk = min(SCAN_CHUNK, seq)

    mod3 = _mod_call(c, cond_w, cond_b).reshape(bsz, 1, -1)

    head_id = jnp.arange(rw) // hd
    bd = (head_id[:, None] == head_id[None, :]).astype(BF16)
    wa2 = jnp.zeros((LANES, 2 * rw), F32)
    wa2 = wa2.at[0:lora_w, 0:rw].set(rw_w2).at[lora_w:, rw:].set(rw_a2)

    rwpack, yconv, gb = _front_call(
        x, mod3, w_in.astype(BF16), mu_shift, conv_w, conv_b, conv_ln_g, conv_ln_b, rw_w0,
        rw_a0, wa2, rw_g2, rw_kk, rw_ka, rw_rk.reshape(-1), bd, min(FRONT_TILE, seq))
    yscan = _scan_call(rwpack, nh, hd, chunk)
    x1, h2, experts, gates = _post_call(
        x, yscan, yconv, gb, mod3, w_out.astype(BF16), rw_lnx_g, rw_lnx_b, ln1_g, ln1_b,
        peer_wq.astype(BF16), peer_k1, peer_k2, bd, hd, alpha, topk, tm)

    ntok = bsz * seq
    ptm = min(PEER_TOK, seq)
    nj = dm // (2 * LANES)
    experts_t = experts.T
    offsets = (experts_t * nj).reshape(-1)
    zpart = _peer_u_call(offsets, h2.reshape(ntok, dm), _pack_call(peer_u), ptm)
    act = _act_call(zpart, gates, ptm)
    out = _peer_v_call(experts_t, act, peer_v.astype(BF16), x1.reshape(ntok, dm), mod3,
                       ln2_g, ln2_b, alpha, peer_k1.shape[1], seq, ptm)
    return out.reshape(bsz, seq, dm)
```

```python
import functools
import math

import jax
import jax.numpy as jnp
from jax import lax
from jax.experimental import pallas as pl
from jax.experimental.pallas import tpu as pltpu

F32 = jnp.float32
BF16 = jnp.bfloat16
HIGHEST = lax.Precision.HIGHEST

LN_EPS = 1e-5
GN_EPS = 64e-5
PEER_TOPK = 16
DEPTH = 1
LANES = 128
SUBLANES = 8
ROW_TILE = 256
FRONT_TILE = 256
ROUTE_HEADS_PER_TRIP = 8
SCAN_CHUNK = 64
SCAN_ROWS = 4
PEER_TOK = 128
U_TOKENS_PER_TRIP = 32
V_TOKENS_PER_TRIP = 64
TILE_PITCH = 136
BF16_ROWS = 16
PACK_ROWS = 512
VMEM_LIMIT = 56 * 1024 * 1024


def _cparams(sem):
    return pltpu.CompilerParams(dimension_semantics=sem, vmem_limit_bytes=VMEM_LIMIT)


def _dot(a, b, precision=None):
    return jnp.dot(a, b, preferred_element_type=F32, precision=precision)


def _sigmoid(x):
    return 1.0 / (1.0 + jnp.exp(-x))


def _ln(x, eps):
    mu = jnp.mean(x, axis=-1, keepdims=True)
    xc = x - mu
    var = jnp.mean(xc * xc, axis=-1, keepdims=True)
    return xc * lax.rsqrt(var + eps)


def _head_sums(x, bd):
    hi = x.astype(BF16)
    lo = (x - hi.astype(F32)).astype(BF16)
    return _dot(hi, bd) + _dot(lo, bd)


def _const_spec(shape):
    nd = len(shape)
    return pl.BlockSpec(shape, lambda *_: (0,) * nd, pipeline_mode=pl.Buffered(1))


def _mod_kernel(c_ref, w_ref, b_ref, o_ref):
    c = c_ref[...]
    o_ref[...] = _dot(c * _sigmoid(c), w_ref[...], HIGHEST) + b_ref[...]


def _mod_call(c, cond_w, cond_b):
    bsz, dm = c.shape
    nblk = cond_w.shape[1] // dm
    return pl.pallas_call(
        _mod_kernel,
        name="mod",
        out_shape=jax.ShapeDtypeStruct((bsz, nblk * dm), F32),
        grid=(nblk,),
        in_specs=[pl.BlockSpec((bsz, dm), lambda j: (0, 0)),
                  pl.BlockSpec((dm, dm), lambda j: (0, j)),
                  pl.BlockSpec((1, dm), lambda j: (0, j))],
        out_specs=pl.BlockSpec((bsz, dm), lambda j: (0, j)),
        compiler_params=_cparams(("arbitrary",)),
    )(c, cond_w, cond_b.reshape(1, -1))


def _front_kernel(x_ref, mod_ref, win_ref, mu_ref, convw_ref, convb_ref, cg_ref, cb_ref,
                  w0_ref, a0_ref, wa2_ref, g2_ref, kkw_ref, kaw_ref, rk_ref, bd_ref,
                  rw_ref, yconv_ref, gb_ref, ubuf_ref, prev_ref, *, dm, cw, rw, ksize, halo):
    tm = x_ref.shape[1]
    first = pl.program_id(1) == 0

    @pl.when(first)
    def _():
        ubuf_ref[0:halo, :] = jnp.zeros((halo, cw), F32)
        prev_ref[...] = jnp.zeros(prev_ref.shape, F32)

    x = x_ref[0]
    mod = mod_ref[0]
    shift1 = mod[:, 0:dm]
    scale1 = mod[:, dm:2 * dm]
    h = _ln(x, LN_EPS) * (1.0 + scale1) + shift1
    p = _dot(h.astype(BF16), win_ref[...])

    u = p[:, 0:cw] * _sigmoid(p[:, cw:2 * cw])
    ubuf_ref[halo:halo + tm, :] = u
    acc = jnp.zeros((tm, cw), F32) + convb_ref[...]
    off = halo - (ksize - 1)
    ub = ubuf_ref[...]
    nrow = tm + halo
    for b in range(SUBLANES):
        xb = ub if b == 0 else pltpu.roll(ub, nrow - b, 0)
        for j in range(ksize):
            if (off + j) % SUBLANES == b:
                a8 = off + j - b
                acc = acc + convw_ref[j:j + 1, :] * xb[a8:a8 + tm, :]
    ubuf_ref[0:halo, :] = ubuf_ref[tm:tm + halo, :]
    yc = _ln(acc, LN_EPS) * cg_ref[...] + cb_ref[...]
    yconv_ref[0] = yc * _sigmoid(yc)

    prw = p[:, 2 * cw:]
    rolled = pltpu.roll(prw, 1, 0)
    row = lax.broadcasted_iota(jnp.int32, prw.shape, 0)
    p_prev = jnp.where(row == 0, prev_ref[0:1, :], rolled)
    prev_ref[0:1, :] = prw[tm - 1:tm, :]
    xm = prw + (p_prev - prw) * mu_ref[...]
    r = xm[:, 0:rw]
    k = xm[:, rw:2 * rw]
    v = xm[:, 2 * rw:3 * rw]
    wa = xm[:, 3 * rw:3 * rw + LANES]
    gd = xm[:, 3 * rw + LANES:3 * rw + 2 * LANES]
    lane = lax.broadcasted_iota(jnp.int32, wa.shape, 1)
    wa = jnp.where(lane < LANES // 2, jnp.tanh(wa), wa)
    t12 = _dot(wa, wa2_ref[...], HIGHEST)
    g = _dot(_sigmoid(gd), g2_ref[...], HIGHEST)
    y = w0_ref[...] + t12[:, 0:rw]
    sp = jnp.maximum(-y, 0.0) + jnp.log(1.0 + jnp.exp(-jnp.abs(y)))
    lw = -jnp.exp(-sp - 0.5)
    a = _sigmoid(a0_ref[...] + t12[:, rw:2 * rw])
    kk = k * kkw_ref[...]
    ss = _head_sums(kk * kk, bd_ref[...])
    kk = kk / jnp.maximum(jnp.sqrt(ss), 1e-12)
    kmod = k * (1.0 + (a - 1.0) * kaw_ref[...])
    bonus = _head_sums(r * kmod * rk_ref[...], bd_ref[...]) * v
    rw_ref[0, :, 0:rw] = r
    rw_ref[0, :, rw:2 * rw] = kmod
    rw_ref[0, :, 2 * rw:3 * rw] = v
    rw_ref[0, :, 3 * rw:4 * rw] = kk
    rw_ref[0, :, 4 * rw:5 * rw] = kk * a
    rw_ref[0, :, 5 * rw:6 * rw] = lw
    gb_ref[0, :, 0:rw] = g
    gb_ref[0, :, rw:2 * rw] = bonus


def _front_call(x, mod3, w_in, mu_shift, conv_w, conv_b, cg, cb, w0, a0, wa2, g2, kkw, kaw,
                rk, bd, tm):
    bsz, seq, dm = x.shape
    ksize, cw = conv_w.shape
    rw = w0.shape[-1]
    halo = 32
    assert ksize - 1 <= halo <= tm
    row = lambda a: a.reshape(1, -1)
    consts = [w_in, row(mu_shift), conv_w, row(conv_b), row(cg), row(cb), row(w0), row(a0),
              wa2, g2, row(kkw), row(kaw), row(rk), bd]
    kern = functools.partial(_front_kernel, dm=dm, cw=cw, rw=rw, ksize=ksize, halo=halo)
    return pl.pallas_call(
        kern,
        name="front",
        out_shape=(jax.ShapeDtypeStruct((bsz, seq, 6 * rw), F32),
                   jax.ShapeDtypeStruct((bsz, seq, cw), F32),
                   jax.ShapeDtypeStruct((bsz, seq, 2 * rw), F32)),
        grid=(bsz, seq // tm),
        in_specs=[pl.BlockSpec((1, tm, dm), lambda b, s: (b, s, 0)),
                  pl.BlockSpec((1, 1, mod3.shape[-1]), lambda b, s: (b, 0, 0))]
                 + [_const_spec(a.shape) for a in consts],
        out_specs=(pl.BlockSpec((1, tm, 6 * rw), lambda b, s: (b, s, 0)),
                   pl.BlockSpec((1, tm, cw), lambda b, s: (b, s, 0)),
                   pl.BlockSpec((1, tm, 2 * rw), lambda b, s: (b, s, 0))),
        scratch_shapes=[pltpu.VMEM((tm + halo, cw), F32),
                        pltpu.VMEM((8, w_in.shape[1] - 2 * cw), F32)],
        compiler_params=_cparams(("arbitrary", "arbitrary")),
    )(x, mod3, *consts)


SCAN_PASSES = 1
_NN = (((2,), (1,)), ((0,), (0,)))
_NT = (((2,), (2,)), ((0,), (0,)))
_TN = (((1,), (1,)), ((0,), (0,)))


def _split(x, passes=3):
    hi = x.astype(BF16)
    if passes == 1:
        return hi, None
    return hi, (x - hi.astype(F32)).astype(BF16)


def _mm3(a, b, dims):
    (ah, al), (bh, bl) = a, b
    dg = lambda p, q: lax.dot_general(p, q, dims, preferred_element_type=F32)
    out = dg(ah, bh)
    if bl is not None:
        out = out + dg(ah, bl)
    if al is not None:
        out = out + dg(al, bh)
    return out


def _scan_kernel(rw_ref, tri_ref, y_ref, state_ref, *, nh, hd):
    L = rw_ref.shape[1]
    rwid = nh * hd

    @pl.when(pl.program_id(1) == 0)
    def _():
        state_ref[...] = jnp.zeros(state_ref.shape, F32)

    nrow = rw_ref.shape[0]
    tri = tri_ref[...]
    parts = [[] for _ in range(6)]
    for b in range(nrow):
        blk = rw_ref[b]
        lw = blk[:, 5 * rwid:6 * rwid]
        c = _dot(tri, lw, HIGHEST)
        e_pos = jnp.exp(c)
        e_neg = jnp.exp(-c)
        parts[0].append(blk[:, 0:rwid] * e_pos)
        parts[1].append(blk[:, rwid:2 * rwid] * e_neg)
        parts[2].append(blk[:, 3 * rwid:4 * rwid] * jnp.exp(c - lw))
        parts[3].append(blk[:, 4 * rwid:5 * rwid] * e_neg)
        parts[4].append(blk[:, 2 * rwid:3 * rwid])
        parts[5].append(e_pos[L - 1:L, :])

    def heads(ts):
        return jnp.stack([t[:, h * hd:(h + 1) * hd] for t in ts for h in range(nh)], axis=0)

    rt, kh, kt, bh, v, plh = (heads(p) for p in parts)
    s0 = state_ref[...]

    strict = (tri - jnp.eye(L, dtype=F32))[None]
    incl = tri[None]
    sp = functools.partial(_split, passes=SCAN_PASSES)
    kt_s, bh_s, kh_s, rt_s, v_s, s0_s = (sp(t) for t in (kt, bh, kh, rt, v, s0))
    t_bb = _mm3(kt_s, bh_s, _NT) * strict
    t_bk = _mm3(kt_s, kh_s, _NT) * strict
    g_rb = _mm3(rt_s, bh_s, _NT) * incl
    g_rk = _mm3(rt_s, kh_s, _NT) * incl

    w = _mm3(kt_s, s0_s, _NT) + _mm3(sp(t_bk), v_s, _NN)
    tp_s = sp(t_bb)
    w = w - _mm3(tp_s, sp(w), _NN)
    n = 2
    while n < L:
        tp_s = sp(_mm3(tp_s, tp_s, _NN))
        w = w + _mm3(tp_s, sp(w), _NN)
        n *= 2
    u_s = sp(w)
    y = (_mm3(rt_s, s0_s, _NT) - _mm3(sp(g_rb), u_s, _NN)
         + _mm3(sp(g_rk), v_s, _NN))
    state_ref[...] = (s0 * plh + _mm3(v_s, sp(kh * plh), _TN)
                      - _mm3(u_s, sp(bh * plh), _TN))
    for b in range(nrow):
        y_ref[b] = jnp.concatenate([y[b * nh + h] for h in range(nh)], axis=-1)


def _scan_call(rwpack, nh, hd, chunk):
    bsz, seq, w6 = rwpack.shape
    rwid = nh * hd
    nrow = SCAN_ROWS if bsz % SCAN_ROWS == 0 else 1
    tri = jnp.tril(jnp.ones((chunk, chunk), F32))
    kern = functools.partial(_scan_kernel, nh=nh, hd=hd)
    return pl.pallas_call(
        kern,
        name="rwkv_scan",
        out_shape=jax.ShapeDtypeStruct((bsz, seq, rwid), F32),
        grid=(bsz // nrow, seq // chunk),
        in_specs=[pl.BlockSpec((nrow, chunk, w6), lambda b, s: (b, s, 0)),
                  _const_spec(tri.shape)],
        out_specs=pl.BlockSpec((nrow, chunk, rwid), lambda b, s: (b, s, 0)),
        scratch_shapes=[pltpu.VMEM((nrow * nh, hd, hd), F32)],
        compiler_params=_cparams(("arbitrary", "arbitrary")),
    )(rwpack, tri)


def _extract_max(s, iota, fill, remove=True):
    m = jnp.max(s, axis=0, keepdims=True)
    idx = jnp.min(jnp.where(s == m, iota, fill), axis=0, keepdims=True)
    return m, idx, (jnp.where(iota == idx, -jnp.inf, s) if remove else s)


def _topk_cols(s, iota, fill, k):
    vals, idxs = [], []
    for r in range(k):
        m, idx, s = _extract_max(s, iota, fill, remove=r + 1 < k)
        vals.append(m)
        idxs.append(idx)
    return jnp.concatenate(vals, axis=0), jnp.concatenate(idxs, axis=0)


def _topk_pair_sums(v1, v2, k):
    rows = lax.broadcasted_iota(jnp.int32, v1.shape, 0)
    base = rows * k
    ptr = jnp.zeros(v1.shape, jnp.int32)
    heads = v1 + v2[0:1, :]
    vals, picks = [], []
    for r in range(k):
        flat = base + ptr
        m, idx, _ = _extract_max(heads, flat, k * k, remove=False)
        vals.append(m)
        picks.append(idx)
        if r + 1 < k:
            hit = flat == idx
            nb = (idx & (k - 1)) + 1
            nxt = jnp.sum(jnp.where(rows == nb, v2, 0.0), axis=0, keepdims=True)
            nxt = jnp.where(nb < k, nxt, -jnp.inf)
            heads = jnp.where(hit, v1 + nxt, heads)
            ptr = jnp.where(hit, ptr + 1, ptr)
    return jnp.concatenate(vals, axis=0), jnp.concatenate(picks, axis=0)


def _take_rows(table, idx):
    out = jnp.zeros(idx.shape, table.dtype)
    for a in range(table.shape[0]):
        out = jnp.where(idx == a, table[a:a + 1, :], out)
    return out


def _post_kernel(x_ref, y_ref, yconv_ref, gb_ref, mod_ref, wout_ref, lnxg_ref, lnxb_ref,
                 ln1g_ref, ln1b_ref, wq_ref, k1_ref, k2_ref, bd_ref,
                 x1_ref, h2_ref, e_ref, gate_ref, q_ref, *, dm, rw, hd, alpha, nkeys, topk):
    tm = x_ref.shape[1]
    nheads = k1_ref.shape[0]
    half = k1_ref.shape[2]
    x = x_ref[0]
    mod = mod_ref[0]
    gate1 = mod[:, 2 * dm:3 * dm]
    shift2 = mod[:, 3 * dm:4 * dm]
    scale2 = mod[:, 4 * dm:5 * dm]

    y = y_ref[0]
    bd = bd_ref[...]
    mu = _head_sums(y, bd) * (1.0 / hd)
    yc = y - mu
    var = _head_sums(yc * yc, bd) * (1.0 / hd)
    gn = yc * lax.rsqrt(var + GN_EPS)
    gb = gb_ref[0]
    y_rw = (gn * lnxg_ref[...] + lnxb_ref[...] + gb[:, rw:2 * rw]) * gb[:, 0:rw]
    ycat = jnp.concatenate([yconv_ref[0], y_rw], axis=-1)
    y1 = _dot(ycat.astype(BF16), wout_ref[...])
    x1 = _ln(alpha * x + gate1 * y1, LN_EPS) * ln1g_ref[...] + ln1b_ref[...]
    x1_ref[0] = x1
    h2 = _ln(x1, LN_EPS) * (1.0 + scale2) + shift2
    h2_ref[0] = h2
    q = _dot(h2.astype(BF16), wq_ref[...])
    for i in range(2 * nheads):
        q_ref[i] = q[:, i * half:(i + 1) * half]

    shift = topk.bit_length() - 1
    assert topk == 1 << shift
    key_iota = lax.broadcasted_iota(jnp.int32, (nkeys, LANES), 0)

    def route(h, c0):
        s1 = lax.dot_general(k1_ref[h], q_ref[2 * h, c0:c0 + LANES, :], (((1,), (1,)), ((), ())),
                             preferred_element_type=F32, precision=HIGHEST)
        s2 = lax.dot_general(k2_ref[h], q_ref[2 * h + 1, c0:c0 + LANES, :],
                             (((1,), (1,)), ((), ())),
                             preferred_element_type=F32, precision=HIGHEST)
        v1, i1 = _topk_cols(s1, key_iota, nkeys, topk)
        v2, i2 = _topk_cols(s2, key_iota, nkeys, topk)
        sc, pick = _topk_pair_sums(v1, v2, topk)
        ex = (_take_rows(i1, lax.shift_right_logical(pick, shift)) * nkeys
              + _take_rows(i2, pick & (topk - 1)))
        pexp = jnp.exp(sc - sc[0:1, :])
        gates = pexp / jnp.sum(pexp, axis=0, keepdims=True)
        base = pl.multiple_of(h * topk, topk)
        e_ref[pl.ds(base, topk), c0:c0 + LANES] = ex
        gate_ref[pl.ds(base, topk), c0:c0 + LANES] = gates

    def head_body(hp, carry):
        for dh in range(ROUTE_HEADS_PER_TRIP):
            for c0 in range(0, tm, LANES):
                route(ROUTE_HEADS_PER_TRIP * hp + dh, c0)
        return carry

    lax.fori_loop(0, nheads // ROUTE_HEADS_PER_TRIP, head_body, 0)


def _post_call(x, yscan, yconv, gb, mod3, w_out, lnxg, lnxb, ln1g, ln1b, wq, k1, k2, bd,
               hd, alpha, topk, tm):
    bsz, seq, dm = x.shape
    rw = yscan.shape[-1]
    cw = yconv.shape[-1]
    nheads, nkeys, half = k1.shape
    nslot = nheads * topk
    ntok = bsz * seq
    nst = seq // tm
    row = lambda a: a.reshape(1, -1)
    consts = [w_out, row(lnxg), row(lnxb), row(ln1g), row(ln1b), wq, k1, k2, bd]
    kern = functools.partial(_post_kernel, dm=dm, rw=rw, hd=hd, alpha=alpha, nkeys=nkeys,
                             topk=topk)
    tok = lambda w: pl.BlockSpec((1, tm, w), lambda b, s: (b, s, 0))
    slot = pl.BlockSpec((nslot, tm), lambda b, s: (0, b * nst + s))
    return pl.pallas_call(
        kern,
        name="post_route",
        out_shape=(jax.ShapeDtypeStruct((bsz, seq, dm), F32),
                   jax.ShapeDtypeStruct((bsz, seq, dm), F32),
                   jax.ShapeDtypeStruct((nslot, ntok), jnp.int32),
                   jax.ShapeDtypeStruct((nslot, ntok), F32)),
        grid=(bsz, nst),
        in_specs=[tok(dm), tok(rw), tok(cw), tok(2 * rw),
                  pl.BlockSpec((1, 1, mod3.shape[-1]), lambda b, s: (b, 0, 0))]
                 + [_const_spec(a.shape) for a in consts],
        out_specs=(tok(dm), tok(dm), slot, slot),
        scratch_shapes=[pltpu.VMEM((2 * nheads, tm, half), F32)],
        compiler_params=_cparams(("arbitrary", "arbitrary")),
    )(x, yscan, yconv, gb, mod3, *consts)


def _pack_kernel(t_ref, o_ref):
    t = t_ref[...]
    hw = t.shape[1] // 2
    hi = lax.bitcast_convert_type(t[:, 0:hw].astype(BF16).astype(F32), jnp.uint32)
    lo = lax.bitcast_convert_type(t[:, hw:].astype(BF16).astype(F32), jnp.uint32)
    o_ref[...] = (hi & jnp.uint32(0xFFFF0000)) | (lo >> 16)


def _pack_call(table):
    ne, dm = table.shape
    packed = pl.pallas_call(
        _pack_kernel,
        name="peer_pack",
        out_shape=jax.ShapeDtypeStruct((ne, dm // 2), jnp.uint32),
        grid=(ne // PACK_ROWS,),
        in_specs=[pl.BlockSpec((PACK_ROWS, dm), lambda i: (i, 0))],
        out_specs=pl.BlockSpec((PACK_ROWS, dm // 2), lambda i: (i, 0)),
        compiler_params=_cparams(("arbitrary",)),
    )(table)
    return packed.reshape(ne * (dm // (2 * LANES)), LANES)


def _unpack(w):
    hi = lax.bitcast_convert_type(w & jnp.uint32(0xFFFF0000), F32)
    lo = lax.bitcast_convert_type(w << 16, F32)
    return hi, lo


def _gather_pair(tbl_ref, e_ref, off, nj):
    rows = [tbl_ref[pl.ds(pl.multiple_of(e_ref[off + d], nj), nj), :] for d in range(2)]
    return jnp.concatenate(rows, axis=0)


def _stack_chunks(row, first, nj):
    sub = lax.broadcasted_iota(jnp.int32, (2 * nj, LANES), 0) % nj
    chunk = lambda c: jnp.broadcast_to(row[:, c * LANES:(c + 1) * LANES], (2 * nj, LANES))
    out = chunk(first)
    for q in range(1, nj):
        out = jnp.where(sub == q, chunk(first + q), out)
    return out


def _peer_u_kernel(e_ref, h_ref, tbl_ref, z_ref, p0_ref, p1_ref, *, nslot, nj):
    tm = h_ref.shape[0]
    rows = 2 * nj
    lane = lax.broadcasted_iota(jnp.int32, (rows, LANES), 1)
    z_ref[...] = jnp.zeros(z_ref.shape, F32)

    @pl.when(pl.program_id(0) == 0)
    def _():
        p1_ref[...] = jnp.zeros(p1_ref.shape, F32)

    def reduce_rows(p_ref, row, tcol):
        rs = jnp.sum(p_ref[row:row + rows, :], axis=1, keepdims=True)
        z_ref[row:row + rows, :] = jnp.where(lane == tcol, rs, z_ref[row:row + rows, :])

    def one_token(t, p_ref, q_ref):
        hrow = h_ref[pl.ds(t, 1), :]
        h_hi = _stack_chunks(hrow, 0, nj)
        h_lo = _stack_chunks(hrow, nj, nj)
        ev = e_ref.at[pl.ds(t * nslot, nslot)]
        for i in range(0, nslot, 2):
            hi, lo = _unpack(_gather_pair(tbl_ref, ev, i, nj))
            p_ref[i * nj:i * nj + rows, :] = hi * h_hi + lo * h_lo
            reduce_rows(q_ref, i * nj, t - 1)

    def tok_body(i, carry):
        for d in range(0, U_TOKENS_PER_TRIP, 2):
            one_token(U_TOKENS_PER_TRIP * i + d, p0_ref, p1_ref)
            one_token(U_TOKENS_PER_TRIP * i + d + 1, p1_ref, p0_ref)
        return carry

    lax.fori_loop(0, tm // U_TOKENS_PER_TRIP, tok_body, 0)
    for r in range(0, nslot * nj, rows):
        reduce_rows(p1_ref, r, tm - 1)


def _peer_u_call(experts, h2, tbl, tm):
    ntok, dm = h2.shape
    nslot = experts.shape[0] // ntok
    nj = dm // (2 * LANES)
    kern = functools.partial(_peer_u_kernel, nslot=nslot, nj=nj)
    return pl.pallas_call(
        kern,
        name="peer_u",
        out_shape=jax.ShapeDtypeStruct((nslot * nj, ntok), F32),
        grid=(ntok // tm,),
        in_specs=[pl.BlockSpec((tm * nslot,), lambda i: (i,), memory_space=pltpu.SMEM),
                  pl.BlockSpec((tm, dm), lambda i: (i, 0)),
                  _const_spec(tbl.shape)],
        out_specs=pl.BlockSpec((nslot * nj, tm), lambda i: (0, i)),
        scratch_shapes=[pltpu.VMEM((nslot * nj, LANES), F32),
                        pltpu.VMEM((nslot * nj, LANES), F32)],
        compiler_params=_cparams(("arbitrary",)),
    )(experts, h2, tbl)


def _act_kernel(z_ref, sel_ref, gate_ref, a_ref):
    zp = z_ref[...]
    zh = zp.astype(BF16)
    zl = (zp - zh.astype(F32)).astype(BF16)
    sel = sel_ref[...]
    z = _dot(sel, zh) + _dot(sel, zl)
    a = 0.5 * z * (1.0 + lax.erf(z * (1.0 / math.sqrt(2.0)))) * gate_ref[...]
    a_ref[...] = a.T


def _act_call(zpart, gates, tm):
    nslot, ntok = gates.shape
    nj = zpart.shape[0] // nslot
    sel = (jnp.arange(nslot * nj)[None, :] // nj == jnp.arange(nslot)[:, None]).astype(BF16)
    return pl.pallas_call(
        _act_kernel,
        name="peer_act",
        out_shape=jax.ShapeDtypeStruct((ntok, nslot), F32),
        grid=(ntok // tm,),
        in_specs=[pl.BlockSpec((nslot * nj, tm), lambda i: (0, i)),
                  _const_spec(sel.shape),
                  pl.BlockSpec((nslot, tm), lambda i: (0, i))],
        out_specs=pl.BlockSpec((tm, nslot), lambda i: (i, 0)),
        compiler_params=_cparams(("arbitrary",)),
    )(zpart, sel, gates)


def _peer_v_kernel(e_ref, act_ref, v_ref, x_ref, mod_ref, g_ref, b_ref, o_ref, a3_ref, a2_ref,
                   *, nslot, nkeys, alpha):
    tm, dm = o_ref.shape
    shift = nkeys.bit_length() - 1
    sub_key = lax.broadcasted_iota(jnp.int32, (nkeys, nslot), 0)

    def one_token(t):
        e_row = e_ref[pl.ds(t, 1), :]
        w_row = act_ref[pl.ds(t, 1), :]
        p1w = jnp.where(sub_key == lax.shift_right_logical(e_row, shift), w_row, 0.0)
        p2t = jnp.where(sub_key == (e_row & (nkeys - 1)), 1.0, 0.0)
        row = pl.multiple_of(t * TILE_PITCH, SUBLANES)
        a3_ref[pl.ds(row, nkeys), :] = lax.dot_general(
            p1w.astype(BF16), p2t.astype(BF16), (((1,), (1,)), ((), ())),
            preferred_element_type=F32)

    def tok_body(i, carry):
        for d in range(V_TOKENS_PER_TRIP):
            one_token(V_TOKENS_PER_TRIP * i + d)
        return carry

    lax.fori_loop(0, tm // V_TOKENS_PER_TRIP, tok_body, 0)

    for r in range(0, tm, BF16_ROWS):
        for i1 in range(nkeys):
            rows = a3_ref[pl.ds(r * TILE_PITCH + i1, BF16_ROWS, stride=TILE_PITCH), :]
            a2_ref[r:r + BF16_ROWS, i1 * nkeys:(i1 + 1) * nkeys] = rows.astype(BF16)
    y = _dot(a2_ref[...], v_ref[...])
    gate2 = mod_ref[0][:, 5 * dm:6 * dm]
    o_ref[...] = _ln(alpha * x_ref[...] + gate2 * y, LN_EPS) * g_ref[...] + b_ref[...]


def _peer_v_call(experts, act, table, x1, mod3, g, b, alpha, nkeys, seq, tm):
    ntok, nslot = act.shape
    ne, dm = table.shape
    assert nslot == LANES and ne == nkeys * nkeys and seq % tm == 0 and TILE_PITCH >= nkeys
    kern = functools.partial(_peer_v_kernel, nslot=nslot, nkeys=nkeys, alpha=alpha)
    tok = pl.BlockSpec((tm, dm), lambda i: (i, 0))
    slots = pl.BlockSpec((tm, nslot), lambda i: (i, 0))
    return pl.pallas_call(
        kern,
        name="peer_v",
        out_shape=jax.ShapeDtypeStruct((ntok, dm), F32),
        grid=(ntok // tm,),
        in_specs=[slots, slots, _const_spec(table.shape), tok,
                  pl.BlockSpec((1, 1, mod3.shape[-1]), lambda i: (i * tm // seq, 0, 0)),
                  _const_spec((1, dm)), _const_spec((1, dm))],
        out_specs=tok,
        scratch_shapes=[pltpu.VMEM((tm * TILE_PITCH, nkeys), F32), pltpu.VMEM((tm, ne), BF16)],
        compiler_params=_cparams(("arbitrary",)),
    )(experts, act, table, x1, mod3, g.reshape(1, -1), b.reshape(1, -1))


def kernel(x, c, cond_w, cond_b, w_in, mu_shift, conv_w, conv_b, conv_ln_g, conv_ln_b, rw_w0, rw_w2, rw_a0, rw_a2, rw_g2, rw_kk, rw_ka, rw_rk, rw_lnx_g, rw_lnx_b, w_out, ln1_g, ln1_b, peer_wq, peer_k1, peer_k2, peer_u, peer_v, ln2_g, ln2_b):
    bsz, seq, dm = x.shape
    nh, hd = rw_rk.shape
    rw = nh * hd
    lora_w, lora_a = rw_w2.shape[0], rw_a2.shape[0]
    assert lora_w == lora_a == LANES // 2 and rw_g2.shape[0] == LANES
    topk = PEER_TOPK
    alpha = (2.0 * DEPTH) ** 0.25
    tm = min(ROW_TILE, seq)
    chunk = min(SCAN_CHUNK, seq)

    mod3 = _mod_call(c, cond_w, cond_b).reshape(bsz, 1, -1)

    head_id = jnp.arange(rw) // hd
    bd = (head_id[:, None] == head_id[None, :]).astype(BF16)
    wa2 = jnp.zeros((LANES, 2 * rw), F32)
    wa2 = wa2.at[0:lora_w, 0:rw].set(rw_w2).at[lora_w:, rw:].set(rw_a2)

    rwpack, yconv, gb = _front_call(
        x, mod3, w_in.astype(BF16), mu_shift, conv_w, conv_b, conv_ln_g, conv_ln_b, rw_w0,
        rw_a0, wa2, rw_g2, rw_kk, rw_ka, rw_rk.reshape(-1), bd, min(FRONT_TILE, seq))
    yscan = _scan_call(rwpack, nh, hd, chunk)
    x1, h2, experts, gates = _post_call(
        x, yscan, yconv, gb, mod3, w_out.astype(BF16), rw_lnx_g, rw_lnx_b, ln1_g, ln1_b,
        peer_wq.astype(BF16), peer_k1, peer_k2, bd, hd, alpha, topk, tm)

    ntok = bsz * seq
    ptm = min(PEER_TOK, seq)
    nj = dm // (2 * LANES)
    experts_t = experts.T
    offsets = (experts_t * nj).reshape(-1)
    zpart = _peer_u_call(offsets, h2.reshape(ntok, dm), _pack_call(peer_u), ptm)
    act = _act_call(zpart, gates, ptm)
    out = _peer_v_call(experts_t, act, peer_v.astype(BF16), x1.reshape(ntok, dm), mod3,
                       ln2_g, ln2_b, alpha, peer_k1.shape[1], seq, ptm)
    return out.reshape(bsz, seq, dm)
```
